```python
import jax, jax.numpy as jnp
from jax import lax
import numpy as np


D_MODEL = 1024
BATCH = 1
SEQ = 16384
DEPTH = 1
DEC_BATCH = 32
DEC_SEQ = 4
PAST_LEN = 16384
PAGE_SIZE = 128

HEAD_DIM = 64
N_HEADS_NSA = 8
N_KV_NSA = 2
GROUP_NSA = N_HEADS_NSA // N_KV_NSA
N_HEADS_SB = 8
ROPE_DIM = HEAD_DIM // 4
ROPE_THETA = 500000.0
CMP_STRIDE = 16
CMP_LEN = 2 * CMP_STRIDE
CMP_HIDDEN = 4 * HEAD_DIM
SEL_BLOCK = 64
SEL_RATIO = SEL_BLOCK // CMP_STRIDE
TOP_N = 16
WINDOW = 512
Q_BLOCK = 128
D_FF = -(-8 * D_MODEL // (3 * 256)) * 256
RMS_EPS = 1e-6
NEG_INF = -1e30
BIG = 1e30
TINY = 1e-30
ATTN_SCALE = HEAD_DIM ** -0.5
Q_NSA = N_HEADS_NSA * HEAD_DIM
KV_NSA_COLS = 6 * N_KV_NSA * HEAD_DIM
GATE_NSA = 3 * N_HEADS_NSA
Q_SB = N_HEADS_SB * HEAD_DIM
KV_SB_COLS = 2 * N_HEADS_SB * HEAD_DIM
MERGE_COLS = 2 * D_MODEL
IN_COLS = Q_NSA + KV_NSA_COLS + GATE_NSA + Q_SB + KV_SB_COLS + MERGE_COLS

kernel_name = 'hybrid_nsa_stickbreaking_decode_step'


def rmsnorm(x, g):
    xf = x.astype(jnp.float32)
    xf = xf * lax.rsqrt(jnp.mean(xf * xf, axis=-1, keepdims=True) + RMS_EPS)
    return (xf * g.astype(jnp.float32)).astype(x.dtype)


def partial_rope(x, pos):
    half = ROPE_DIM // 2
    inv_freq = ROPE_THETA ** (-jnp.arange(half, dtype=jnp.float32) / half)
    ang = pos.astype(jnp.float32)[:, None] * inv_freq[None, :]
    shape = (1, pos.shape[0]) + (1,) * (x.ndim - 3) + (half,)
    cos = jnp.cos(ang).reshape(shape).astype(x.dtype)
    sin = jnp.sin(ang).reshape(shape).astype(x.dtype)
    x1, x2 = x[..., :half], x[..., half:ROPE_DIM]
    return jnp.concatenate([x1 * cos - x2 * sin, x2 * cos + x1 * sin, x[..., ROPE_DIM:]], axis=-1)


def masked_softmax(s, mask):
    s = jnp.where(mask, s, NEG_INF)
    m = jnp.max(s, axis=-1, keepdims=True)
    e = jnp.where(mask, jnp.exp(s - m), 0.0)
    return e / jnp.maximum(jnp.sum(e, axis=-1, keepdims=True), TINY)


def project_in(h, w_in, pos):
    B, T, _ = h.shape
    cuts = [int(v) for v in np.cumsum([Q_NSA, KV_NSA_COLS, GATE_NSA, Q_SB, KV_SB_COLS])]
    q_a, kv_a, g_a, q_b, kv_b, merge = jnp.split(h @ w_in, cuts, axis=-1)
    q_a = partial_rope(q_a.reshape(B, T, N_KV_NSA, GROUP_NSA, HEAD_DIM), pos)
    kv_a = kv_a.reshape(B, T, 6, N_KV_NSA, HEAD_DIM)
    kv_a = kv_a.at[:, :, 0::2].set(partial_rope(kv_a[:, :, 0::2], pos))
    g_a = jax.nn.sigmoid(g_a.astype(jnp.float32)).astype(h.dtype).reshape(B, T, N_KV_NSA, GROUP_NSA, 3)
    q_b = q_b.reshape(B, T, N_HEADS_SB, HEAD_DIM)
    kv_b = kv_b.reshape(B, T, 2, N_HEADS_SB, HEAD_DIM)
    merge = merge.reshape(B, T, 2, D_MODEL)
    return q_a, kv_a[:, :, :4], kv_a[:, :, 4:], g_a, q_b, kv_b, merge


def compress_kv(rows, cmp_pe, cmp_w1, cmp_b1, cmp_w2, cmp_b2):
    B, L = rows.shape[:2]
    n_half = L // CMP_STRIDE
    halves = rows[:, :n_half * CMP_STRIDE].reshape(B, n_half, CMP_STRIDE, 2, N_KV_NSA, HEAD_DIM)
    w1 = cmp_w1.reshape(2, CMP_LEN, HEAD_DIM, CMP_HIDDEN)
    h_lo = jnp.einsum('bnpcgd,cpdh->bncgh', halves, w1[:, :CMP_STRIDE])
    h_hi = jnp.einsum('bnpcgd,cpdh->bncgh', halves, w1[:, CMP_STRIDE:])
    bias = jnp.einsum('cpd,cpdh->ch', cmp_pe, w1) + cmp_b1
    pre = h_lo[:, :-1] + h_hi[:, 1:] + bias[:, None, :]
    out = jnp.einsum('bncgh,chd->bncgd', jax.nn.gelu(pre), cmp_w2) + cmp_b2[:, None, :]
    cmp_end = jnp.arange(n_half - 1) * CMP_STRIDE + CMP_LEN - 1
    return out[:, :, 0], out[:, :, 1], cmp_end


def nsa_core(q, gates, k_cmp, v_cmp, cmp_end, k_slc, v_slc, k_win, v_win, win_pos, q_pos):
    B, T = q.shape[:2]
    s_c = jnp.einsum('btgrd,bngd->bgrtn', q, k_cmp).astype(jnp.float32) * ATTN_SCALE
    p_c = masked_softmax(s_c, cmp_end[None, :] <= q_pos[:, None])
    o_c = jnp.einsum('bgrtn,bngd->btgrd', p_c.astype(v_cmp.dtype), v_cmp)
    n_cmp = k_cmp.shape[1]
    n_sel = k_slc.shape[1] // SEL_BLOCK
    front = CMP_LEN // CMP_STRIDE - 1
    taps = SEL_RATIO + front
    total = SEL_RATIO * (n_sel - 1) + taps
    imp = jnp.pad(p_c.sum(axis=2), ((0, 0), (0, 0), (0, 0), (front, total - front - n_cmp)))
    imp = sum(imp[..., o:o + SEL_RATIO * (n_sel - 1) + 1:SEL_RATIO] for o in range(taps))
    blk = jnp.arange(n_sel)[None, :]
    cur = (q_pos // SEL_BLOCK)[:, None]
    forced = (blk == 0) | (blk == cur) | (blk == cur - 1)
    score = jnp.where(forced, BIG, jnp.where(blk <= cur, imp, NEG_INF))
    top_val, top_idx = lax.top_k(score, min(TOP_N, n_sel))
    tok = (top_idx[..., None] * SEL_BLOCK + jnp.arange(SEL_BLOCK)).reshape(B, N_KV_NSA, T, -1)
    tok_ok = jnp.repeat(top_val > 0.5 * NEG_INF, SEL_BLOCK, axis=-1) & (tok <= q_pos[None, None, :, None])
    bi = jnp.arange(B)[:, None, None, None]
    gi = jnp.arange(N_KV_NSA)[None, :, None, None]
    k_sel = k_slc[bi, tok, gi]
    v_sel = v_slc[bi, tok, gi]
    s_s = jnp.einsum('btgrd,bgtkd->bgrtk', q, k_sel).astype(jnp.float32) * ATTN_SCALE
    p_s = masked_softmax(s_s, tok_ok[:, :, None])
    o_s = jnp.einsum('bgrtk,bgtkd->btgrd', p_s.astype(v_sel.dtype), v_sel)
    s_w = jnp.einsum('btgrd,bsgd->bgrts', q, k_win).astype(jnp.float32) * ATTN_SCALE
    dist = q_pos[:, None] - win_pos[None, :]
    mask_w = (dist >= 0) & (dist < WINDOW) & (win_pos[None, :] >= 0)
    o_w = jnp.einsum('bgrts,bsgd->btgrd', masked_softmax(s_w, mask_w).astype(v_win.dtype), v_win)
    o = gates[..., 0:1] * o_c + gates[..., 1:2] * o_s + gates[..., 2:3] * o_w
    return o.reshape(B, T, Q_NSA)


def stick_breaking(q, k, v, q_pos, k_pos):
    z = jnp.einsum('bthd,bshd->bhts', q, k).astype(jnp.float32) * ATTN_SCALE
    causal = k_pos[None, :] < q_pos[:, None]
    log_1mb = jnp.where(causal, jax.nn.log_sigmoid(-z), 0.0)
    after = lax.cumsum(log_1mb, axis=3, reverse=True) - log_1mb
    a = jnp.where(causal, jnp.exp(jax.nn.log_sigmoid(z) + after), 0.0)
    o = jnp.einsum('bhts,bshd->bthd', a.astype(v.dtype), v)
    return o.reshape(q.shape[0], q.shape[1], Q_SB)


def merge_branches(o_a, o_b, merge, w_o_nsa, w_o_sb, w_out):
    g = jax.nn.sigmoid(merge.astype(jnp.float32)).astype(o_a.dtype)
    return (g[:, :, 0] * (o_a @ w_o_nsa) + g[:, :, 1] * (o_b @ w_o_sb)) @ w_out


def prompt_mixer(h, w_in, cmp_pe, cmp_w1, cmp_b1, cmp_w2, cmp_b2, w_o_nsa, w_o_sb, w_out):
    B, T, _ = h.shape
    pos = jnp.arange(T)
    q_a, nsa_rows, win_rows, g_a, q_b, sb_rows, merge = project_in(h, w_in, pos)
    k_cmp, v_cmp, cmp_end = compress_kv(nsa_rows[:, :, :2], cmp_pe, cmp_w1, cmp_b1, cmp_w2, cmp_b2)
    k_slc, v_slc = nsa_rows[:, :, 2], nsa_rows[:, :, 3]
    win_pad = jnp.pad(win_rows, ((0, 0), (WINDOW, 0), (0, 0), (0, 0), (0, 0)))
    k_sb, v_sb = sb_rows[:, :, 0], sb_rows[:, :, 1]

    def query_block(b):
        start = b * Q_BLOCK
        q_pos = start + jnp.arange(Q_BLOCK)
        win_blk = lax.dynamic_slice_in_dim(win_pad, start, WINDOW + Q_BLOCK, axis=1)
        win_pos = start - WINDOW + jnp.arange(WINDOW + Q_BLOCK)
        o_a = nsa_core(lax.dynamic_slice_in_dim(q_a, start, Q_BLOCK, axis=1),
                       lax.dynamic_slice_in_dim(g_a, start, Q_BLOCK, axis=1),
                       k_cmp, v_cmp, cmp_end, k_slc, v_slc, win_blk[:, :, 0], win_blk[:, :, 1], win_pos, q_pos)
        o_b = stick_breaking(lax.dynamic_slice_in_dim(q_b, start, Q_BLOCK, axis=1), k_sb, v_sb, q_pos, pos)
        return o_a, o_b

    o_a, o_b = lax.map(query_block, jnp.arange(T // Q_BLOCK))
    o_a = jnp.moveaxis(o_a, 0, 1).reshape(B, T, Q_NSA)
    o_b = jnp.moveaxis(o_b, 0, 1).reshape(B, T, Q_SB)
    y = merge_branches(o_a, o_b, merge, w_o_nsa, w_o_sb, w_out)
    return y, (nsa_rows, sb_rows, win_rows[:, T - min(WINDOW, T):])


def sample_mixer(h, cache_nsa, cache_sb, win_state, page_table, w_in, cmp_pe, cmp_w1, cmp_b1, cmp_w2, cmp_b2,
                 w_o_nsa, w_o_sb, w_out):
    B, T, _ = h.shape
    past_len = page_table.shape[1] * PAGE_SIZE
    L = past_len + T
    pos = past_len + jnp.arange(T)
    q_a, nsa_new, win_new, g_a, q_b, sb_new, merge = project_in(h, w_in, pos)
    nsa_rows = jnp.concatenate([cache_nsa[page_table].reshape((B, past_len) + cache_nsa.shape[2:]), nsa_new], axis=1)
    k_cmp, v_cmp, cmp_end = compress_kv(nsa_rows[:, :, :2], cmp_pe, cmp_w1, cmp_b1, cmp_w2, cmp_b2)
    n_sel = -(-L // SEL_BLOCK)
    slc = jnp.pad(nsa_rows[:, :, 2:], ((0, 0), (0, n_sel * SEL_BLOCK - L), (0, 0), (0, 0), (0, 0)))
    wb = win_state.shape[1]
    win_rows = jnp.concatenate([win_state, win_new], axis=1)
    win_pos = jnp.concatenate([past_len - wb + jnp.arange(wb), pos])
    o_a = nsa_core(q_a, g_a, k_cmp, v_cmp, cmp_end, slc[:, :, 0], slc[:, :, 1],
                   win_rows[:, :, 0], win_rows[:, :, 1], win_pos, pos)
    sb_rows = jnp.concatenate([cache_sb[page_table].reshape((B, past_len) + cache_sb.shape[2:]), sb_new], axis=1)
    o_b = stick_breaking(q_b, sb_rows[:, :, 0], sb_rows[:, :, 1], pos, jnp.arange(L))
    y = merge_branches(o_a, o_b, merge, w_o_nsa, w_o_sb, w_out)
    return y, (nsa_new, sb_new, win_rows[:, L - past_len - T + T:][:, -wb:])


def swiglu(h, w_up, w_down):
    gate, up = jnp.split(h @ w_up, 2, axis=-1)
    return (jax.nn.silu(gate) * up) @ w_down


def sandwich_layer(x, c, mixer, ada_w, ada_b, norm_g, ffn_w_up, ffn_w_down):
    mod = (jax.nn.silu(c) @ ada_w + ada_b).reshape(c.shape[0], 6, 1, D_MODEL)
    h = rmsnorm(x, norm_g[0]) * (1.0 + mod[:, 1]) + mod[:, 0]
    o, state = mixer(h)
    x = x + mod[:, 2] * rmsnorm(o, norm_g[1])
    h = rmsnorm(x, norm_g[2]) * (1.0 + mod[:, 4]) + mod[:, 3]
    x = x + mod[:, 5] * rmsnorm(swiglu(h, ffn_w_up, ffn_w_down), norm_g[3])
    return x, state


def setup_inputs(seed: int = 0) -> dict:
    key = jax.random.key(seed)
    ks = jax.random.split(key, 24)
    n_pages = PAST_LEN // PAGE_SIZE
    n_used = DEC_BATCH * n_pages
    n_pool = n_used + max(1, n_used // 4)
    page_table = jax.random.permutation(ks[0], n_pool)[:n_used].reshape(DEC_BATCH, n_pages).astype(jnp.int32)
    nrm = lambda k, shape, s: jax.random.normal(k, shape, jnp.float32) * s
    return {
        'x_prompt': nrm(ks[1], (BATCH, SEQ, D_MODEL), 1.0),
        'x_sample': nrm(ks[2], (DEC_BATCH, DEC_SEQ, D_MODEL), 1.0),
        'cache_nsa_kv': nrm(ks[3], (DEPTH, n_pool, PAGE_SIZE, 4, N_KV_NSA, HEAD_DIM), 1.0),
        'cache_sb_kv': nrm(ks[4], (DEPTH, n_pool, PAGE_SIZE, 2, N_HEADS_SB, HEAD_DIM), 1.0),
        'state_win_kv': nrm(ks[5], (DEPTH, DEC_BATCH, min(WINDOW, PAST_LEN), 2, N_KV_NSA, HEAD_DIM), 1.0),
        'page_table': page_table,
        'c_prompt': nrm(ks[6], (BATCH, D_MODEL), 1.0),
        'c_sample': nrm(ks[7], (DEC_BATCH, D_MODEL), 1.0),
        'ada_w': nrm(ks[8], (DEPTH, D_MODEL, 6 * D_MODEL), 0.5 * D_MODEL ** -0.5),
        'ada_b': nrm(ks[9], (DEPTH, 6 * D_MODEL), 0.02),
        'norm_g': 1.0 + nrm(ks[10], (DEPTH, 4, D_MODEL), 0.05),
        'w_in': nrm(ks[11], (DEPTH, D_MODEL, IN_COLS), D_MODEL ** -0.5),
        'cmp_pe': nrm(ks[12], (DEPTH, 2, CMP_LEN, HEAD_DIM), 0.1),
        'cmp_w1': nrm(ks[13], (DEPTH, 2, CMP_LEN * HEAD_DIM, CMP_HIDDEN), (CMP_LEN * HEAD_DIM) ** -0.5),
        'cmp_b1': nrm(ks[14], (DEPTH, 2, CMP_HIDDEN), 0.02),
        'cmp_w2': nrm(ks[15], (DEPTH, 2, CMP_HIDDEN, HEAD_DIM), CMP_HIDDEN ** -0.5),
        'cmp_b2': nrm(ks[16], (DEPTH, 2, HEAD_DIM), 0.02),
        'w_o_nsa': nrm(ks[17], (DEPTH, Q_NSA, D_MODEL), Q_NSA ** -0.5),
        'w_o_sb': nrm(ks[18], (DEPTH, Q_SB, D_MODEL), Q_SB ** -0.5),
        'w_out': nrm(ks[19], (DEPTH, D_MODEL, D_MODEL), D_MODEL ** -0.5),
        'ffn_w_up': nrm(ks[20], (DEPTH, D_MODEL, 2 * D_FF), D_MODEL ** -0.5),
        'ffn_w_down': nrm(ks[21], (DEPTH, D_FF, D_MODEL), D_FF ** -0.5),
    }


def reference(x_prompt, x_sample, cache_nsa_kv, cache_sb_kv, state_win_kv, page_table, c_prompt, c_sample,
              ada_w, ada_b, norm_g, w_in, cmp_pe, cmp_w1, cmp_b1, cmp_w2, cmp_b2, w_o_nsa, w_o_sb, w_out,
              ffn_w_up, ffn_w_down):
    xp, xs = x_prompt, x_sample
    st_prompt, st_sample = [], []
    for layer in range(DEPTH):
        mw = (w_in[layer], cmp_pe[layer], cmp_w1[layer], cmp_b1[layer], cmp_w2[layer], cmp_b2[layer],
              w_o_nsa[layer], w_o_sb[layer], w_out[layer])
        lw = (ada_w[layer], ada_b[layer], norm_g[layer], ffn_w_up[layer], ffn_w_down[layer])
        xp, sp = sandwich_layer(xp, c_prompt, lambda h: prompt_mixer(h, *mw), *lw)
        xs, ss = sandwich_layer(
            xs, c_sample,
            lambda h: sample_mixer(h, cache_nsa_kv[layer], cache_sb_kv[layer], state_win_kv[layer], page_table, *mw),
            *lw)
        st_prompt.append(sp)
        st_sample.append(ss)
    nsa_kv_prompt = jnp.stack([s[0] for s in st_prompt])
    sb_kv_prompt = jnp.stack([s[1] for s in st_prompt])
    win_kv_prompt = jnp.stack([s[2] for s in st_prompt])
    nsa_kv_sample = jnp.stack([s[0] for s in st_sample])
    sb_kv_sample = jnp.stack([s[1] for s in st_sample])
    win_kv_sample = jnp.stack([s[2] for s in st_sample])
    return (xp, xs, nsa_kv_prompt, sb_kv_prompt, win_kv_prompt, nsa_kv_sample, sb_kv_sample, win_kv_sample)
```

```python
import functools

import numpy as np
import jax
import jax.numpy as jnp
from jax import lax
from jax.experimental import pallas as pl
from jax.experimental.pallas import tpu as pltpu

F32 = jnp.float32
BF16 = jnp.bfloat16
I32 = jnp.int32

D_MODEL = 1024
HEAD_DIM = 64
N_HEADS_NSA = 8
N_KV_NSA = 2
GROUP_NSA = N_HEADS_NSA // N_KV_NSA
N_HEADS_SB = 8
ROPE_DIM = HEAD_DIM // 4
ROPE_THETA = 500000.0
CMP_STRIDE = 16
CMP_LEN = 2 * CMP_STRIDE
CMP_HIDDEN = 4 * HEAD_DIM
SEL_BLOCK = 64
SEL_RATIO = SEL_BLOCK // CMP_STRIDE
TOP_N = 16
WINDOW = 512
PAGE = 128
RMS_EPS = 1e-6
NEG_INF = -1e30
BIG = 1e30
TINY = 1e-30
ATTN_SCALE = HEAD_DIM ** -0.5

LANES = 128
VMEM_LIMIT = 56 * 1024 * 1024
SB_EXIT = -104.0

C_QA = 0
C_KVA = 512
C_GA = 1280
C_QB = 1536
C_KVB = 2048
C_MG = 3072
C_END = 5120


def _cparams(sem):
    return pltpu.CompilerParams(dimension_semantics=sem, vmem_limit_bytes=VMEM_LIMIT)


def _rms(x):
    return x * lax.rsqrt(jnp.mean(x * x, axis=-1, keepdims=True) + RMS_EPS)


def _dot(a, b):
    return jnp.dot(a, b, preferred_element_type=F32)


def _dot_nt(a, b):
    return lax.dot_general(a, b, (((1,), (1,)), ((), ())), preferred_element_type=F32)


def _split3(x):
    p1 = x.astype(BF16)
    r1 = x - p1.astype(F32)
    p2 = r1.astype(BF16)
    p3 = (r1 - p2.astype(F32)).astype(BF16)
    return p1, p2, p3


def _softplus(z):
    return jnp.maximum(z, 0.0) + jnp.log(1.0 + jnp.exp(-jnp.abs(z)))


def _mod_kernel(c_ref, w_ref, b_ref, o_ref):
    c = c_ref[...]
    s = c * jax.nn.sigmoid(c)
    o_ref[...] = _dot(s.astype(BF16), w_ref[...].astype(BF16)) + b_ref[...]


def _mod_call(c, w, b):
    rows, n = c.shape[0], w.shape[1]
    tn = 1536
    return pl.pallas_call(
        _mod_kernel,
        grid=(n // tn,),
        in_specs=[pl.BlockSpec((rows, D_MODEL), lambda j: (0, 0)),
                  pl.BlockSpec((D_MODEL, tn), lambda j: (0, j)),
                  pl.BlockSpec((1, tn), lambda j: (0, j))],
        out_specs=pl.BlockSpec((rows, tn), lambda j: (0, j)),
        out_shape=jax.ShapeDtypeStruct((rows, n), F32),
        compiler_params=_cparams(("arbitrary",)),
        name="adaln_mod",
    )(c, w, b)


def _rope(y, cos, sin):
    w = y.shape[1]
    reps = w // LANES
    if reps > 1:
        cos = jnp.concatenate([cos] * reps, axis=1)
        sin = jnp.concatenate([sin] * reps, axis=1)
    lane = lax.broadcasted_iota(I32, y.shape, 1) & (HEAD_DIM - 1)
    half = ROPE_DIM // 2
    partner = jnp.where(lane < half, pltpu.roll(y, w - half, 1), pltpu.roll(y, half, 1))
    return y * cos + partner * sin


def _proj_kernel(x_ref, shift_ref, scale_ref, ng_ref, w_ref, cos_ref, sin_ref,
                 qa_ref, nsa_ref, win_ref, kslc_ref, vslc_ref, kwin_ref, vwin_ref, gate_ref,
                 qb_ref, sb_ref, ksb_ref, vsb_ref, mg_ref):
    x = x_ref[...]
    h = _rms(x) * ng_ref[...]
    h = h * (1.0 + scale_ref[...]) + shift_ref[...]
    hb = h.astype(BF16)
    cos = cos_ref[...]
    sin = sin_ref[...]

    def mm(c0, c1):
        return _dot(hb, w_ref[:, c0:c1])

    def heads(dst, y, n):
        for i in range(n):
            dst[i] = y[:, i * HEAD_DIM:(i + 1) * HEAD_DIM].astype(dst.dtype)

    qa = _rope(mm(C_QA, C_QA + 512), cos, sin) * ATTN_SCALE
    heads(qa_ref, qa, N_HEADS_NSA)

    kc = _rope(mm(C_KVA, C_KVA + 128), cos, sin)
    vc = mm(C_KVA + 128, C_KVA + 256)
    ks = _rope(mm(C_KVA + 256, C_KVA + 384), cos, sin)
    vs = mm(C_KVA + 384, C_KVA + 512)
    kw = _rope(mm(C_KVA + 512, C_KVA + 640), cos, sin)
    vw = mm(C_KVA + 640, C_KVA + 768)
    nsa_ref[:, 0:128] = kc
    nsa_ref[:, 128:256] = vc
    nsa_ref[:, 256:384] = ks
    nsa_ref[:, 384:512] = vs
    win_ref[:, 0:128] = kw
    win_ref[:, 128:256] = vw
    heads(kslc_ref, ks, N_KV_NSA)
    heads(vslc_ref, vs, N_KV_NSA)
    heads(kwin_ref, kw, N_KV_NSA)
    heads(vwin_ref, vw, N_KV_NSA)

    gate_ref[...] = jax.nn.sigmoid(mm(C_GA, C_GA + 256))

    qb = mm(C_QB, C_QB + 512) * ATTN_SCALE
    heads(qb_ref, qb, N_HEADS_SB)

    kb = mm(C_KVB, C_KVB + 512)
    vb = mm(C_KVB + 512, C_KVB + 1024)
    sb_ref[:, 0:512] = kb
    sb_ref[:, 512:1024] = vb
    heads(ksb_ref, kb, N_HEADS_SB)
    heads(vsb_ref, vb, N_HEADS_SB)

    for j in range(2):
        mg_ref[:, j * 1024:(j + 1) * 1024] = jax.nn.sigmoid(
            mm(C_MG + j * 1024, C_MG + (j + 1) * 1024)).astype(BF16)


def _proj_call(x, shift, scale, ng, w, cos, sin, tm):
    rows = x.shape[0]
    per_row = shift.shape[0] != 1
    mod_spec = (pl.BlockSpec((tm, D_MODEL), lambda i: (i, 0)) if per_row
                else pl.BlockSpec((1, D_MODEL), lambda i: (0, 0)))
    row = lambda n: pl.BlockSpec((tm, n), lambda i: (i, 0))
    hm = lambda n: pl.BlockSpec((n, tm, HEAD_DIM), lambda i: (0, i, 0))
    sds = jax.ShapeDtypeStruct
    out_shape = (
        sds((N_HEADS_NSA, rows, HEAD_DIM), BF16),
        sds((rows, 512), F32),
        sds((rows, 256), F32),
        sds((N_KV_NSA, rows, HEAD_DIM), BF16),
        sds((N_KV_NSA, rows, HEAD_DIM), BF16),
        sds((N_KV_NSA, rows, HEAD_DIM), BF16),
        sds((N_KV_NSA, rows, HEAD_DIM), BF16),
        sds((rows, 256), F32),
        sds((N_HEADS_SB, rows, HEAD_DIM), BF16),
        sds((rows, 1024), F32),
        sds((N_HEADS_SB, rows, HEAD_DIM), BF16),
        sds((N_HEADS_SB, rows, HEAD_DIM), BF16),
        sds((rows, 2048), BF16),
    )
    out_specs = (hm(8), row(512), row(256), hm(2), hm(2), hm(2), hm(2), row(256),
                 hm(8), row(1024), hm(8), hm(8), row(2048))
    return pl.pallas_call(
        _proj_kernel,
        grid=(rows // tm,),
        in_specs=[row(D_MODEL), mod_spec, mod_spec,
                  pl.BlockSpec((1, D_MODEL), lambda i: (0, 0)),
                  pl.BlockSpec((D_MODEL, C_END), lambda i: (0, 0)),
                  row(LANES), row(LANES)],
        out_specs=out_specs,
        out_shape=out_shape,
        compiler_params=_cparams(("arbitrary",)),
        name="proj_in",
    )(x, shift, scale, ng, w, cos, sin)


def _compress_kernel(pt_ref, pages_hbm, w1_ref, bias_ref, w2_ref, b2_ref, kc_ref, vc_ref,
                     buf, sem, hlo, hhi, *, n_pages, cp):
    b = pl.program_id(0)
    nch = n_pages // cp
    nh = cp * (PAGE // CMP_STRIDE)

    def copy(page, slot, k, c):
        return pltpu.make_async_copy(pages_hbm.at[page, :, pl.ds(c * LANES, LANES)],
                                     buf.at[slot, c, pl.ds(k * PAGE, PAGE)], sem.at[slot])

    def start(ch, slot):
        for k in range(cp):
            page = pt_ref[b * n_pages + ch * cp + k]
            for c in range(2):
                copy(page, slot, k, c).start()

    def wait(slot):
        for k in range(cp):
            for c in range(2):
                copy(0, slot, k, c).wait()

    start(0, 0)
    for ch in range(nch):
        slot = ch % 2
        if ch + 1 < nch:
            start(ch + 1, 1 - slot)
        wait(slot)
        for cg in range(4):
            c, g = cg // N_KV_NSA, cg % N_KV_NSA
            acc = jnp.zeros((nh, 2 * CMP_HIDDEN), F32)
            for p in range(CMP_STRIDE):
                xp = buf[slot, c, pl.ds(p, nh, stride=CMP_STRIDE), :]
                acc = acc + _dot(xp[:, g * HEAD_DIM:(g + 1) * HEAD_DIM].astype(BF16), w1_ref[c, p])
            hlo[pl.ds(ch * nh, nh), cg * CMP_HIDDEN:(cg + 1) * CMP_HIDDEN] = acc[:, :CMP_HIDDEN]
            hhi[pl.ds(ch * nh, nh), cg * CMP_HIDDEN:(cg + 1) * CMP_HIDDEN] = acc[:, CMP_HIDDEN:]
    n_half = nch * nh
    hhi[pl.ds(n_half, 8), :] = jnp.zeros((8, 4 * CMP_HIDDEN), F32)
    pre = hlo[...] + hhi[pl.ds(1, n_half), :] + bias_ref[...]
    act = jax.nn.gelu(pre).astype(BF16)
    out = _dot(act, w2_ref[...]) + b2_ref[...]
    for g in range(N_KV_NSA):
        kc_ref[g] = out[:, g * HEAD_DIM:(g + 1) * HEAD_DIM].astype(BF16)
        vc_ref[g] = out[:, (2 + g) * HEAD_DIM:(3 + g) * HEAD_DIM].astype(BF16)


def _compress_call(page_table, pages, w1, bias, w2, b2):
    nb, n_pages = page_table.shape
    cp = min(32, n_pages)
    assert n_pages % cp == 0
    n_half = n_pages * (PAGE // CMP_STRIDE)
    kern = functools.partial(_compress_kernel, n_pages=n_pages, cp=cp)
    const = lambda shape: pl.BlockSpec(shape, lambda b, pt: (0,) * len(shape))
    out_spec = pl.BlockSpec((None, N_KV_NSA, n_half, HEAD_DIM), lambda b, pt: (b, 0, 0, 0))
    grid_spec = pltpu.PrefetchScalarGridSpec(
        num_scalar_prefetch=1,
        grid=(nb,),
        in_specs=[pl.BlockSpec(memory_space=pl.ANY),
                  const((2, CMP_STRIDE, HEAD_DIM, 2 * CMP_HIDDEN)),
                  const((1, 4 * CMP_HIDDEN)),
                  const((4 * CMP_HIDDEN, 4 * HEAD_DIM)),
                  const((1, 4 * HEAD_DIM))],
        out_specs=(out_spec, out_spec),
        scratch_shapes=[pltpu.VMEM((2, 2, cp * PAGE, LANES), F32),
                        pltpu.SemaphoreType.DMA((2,)),
                        pltpu.VMEM((n_half, 4 * CMP_HIDDEN), F32),
                        pltpu.VMEM((n_half + 8, 4 * CMP_HIDDEN), F32)],
    )
    sds = jax.ShapeDtypeStruct((nb, N_KV_NSA, n_half, HEAD_DIM), BF16)
    return pl.pallas_call(
        kern, grid_spec=grid_spec, out_shape=(sds, sds),
        compiler_params=_cparams(("arbitrary",)),
        name="compress_kv",
    )(page_table.reshape(-1), pages, w1, bias, w2, b2)


def _compress_weights(cmp_pe, cmp_w1, cmp_b1, cmp_w2, cmp_b2):
    w1r = cmp_w1.reshape(2, CMP_LEN, HEAD_DIM, CMP_HIDDEN)
    w1 = jnp.concatenate([w1r[:, :CMP_STRIDE], w1r[:, CMP_STRIDE:]], axis=-1).astype(BF16)
    bias = jnp.einsum('cpd,cpdh->ch', cmp_pe, w1r, precision=lax.Precision.HIGHEST) + cmp_b1
    bias = jnp.repeat(bias, N_KV_NSA, axis=0).reshape(1, 4 * CMP_HIDDEN)
    w2 = jnp.zeros((4, CMP_HIDDEN, 4, HEAD_DIM), F32)
    for cg in range(4):
        w2 = w2.at[cg, :, cg, :].set(cmp_w2[cg // 2])
    w2 = w2.reshape(4 * CMP_HIDDEN, 4 * HEAD_DIM).astype(BF16)
    b2 = jnp.repeat(cmp_b2, N_KV_NSA, axis=0).reshape(1, 4 * HEAD_DIM)
    return w1, bias, w2, b2


def _masked_softmax(s, mask):
    s = jnp.where(mask, s, NEG_INF)
    m = jnp.max(s, axis=-1, keepdims=True)
    e = jnp.where(mask, jnp.exp(s - m), 0.0)
    return e / jnp.maximum(jnp.sum(e, axis=-1, keepdims=True), TINY)


def _importance(p_sum, n_cmp_pad, n_sel_pad):
    n = lax.broadcasted_iota(I32, (n_cmp_pad, n_sel_pad), 0)
    j = lax.broadcasted_iota(I32, (n_cmp_pad, n_sel_pad), 1)
    front = CMP_LEN // CMP_STRIDE - 1
    taps = jnp.where((n >= SEL_RATIO * j - front) & (n <= SEL_RATIO * j + SEL_RATIO - 1), 1.0, 0.0).astype(BF16)
    p1, p2, p3 = _split3(p_sum)
    return _dot(p1, taps) + _dot(p2, taps) + _dot(p3, taps)


def _topk_rounds(score, k):
    rows, n = score.shape
    blk = lax.broadcasted_iota(I32, (rows, n), 1).astype(F32)
    outs = []
    work = score
    for _ in range(k):
        mx = jnp.max(work, axis=-1, keepdims=True)
        idx = jnp.min(jnp.where(work == mx, blk, float(n)), axis=-1, keepdims=True)
        outs.append((mx, idx))
        work = jnp.where(blk == idx, -jnp.inf, work)
    return outs


def _block_scores(imp, q_pos, n_sel):
    blk = lax.broadcasted_iota(I32, imp.shape, 1)
    cur = q_pos // SEL_BLOCK
    forced = (blk == 0) | (blk == cur) | (blk == cur - 1)
    score = jnp.where(forced, BIG, jnp.where(blk <= cur, imp, NEG_INF))
    return jnp.where(blk < n_sel, score, -jnp.inf)


def _nsa_prompt_kernel(q_ref, kc_ref, vc_ref, ks_ref, vs_ref, kw_ref, vw_ref, gate_ref, o_ref,
                       *, qb, kb, n_cmp, n_sel, topk):
    i = pl.program_id(1)
    n_cmp_pad = kc_ref.shape[0]
    n_sel_pad = ((n_sel + LANES - 1) // LANES) * LANES
    q = q_ref[...].reshape(GROUP_NSA * qb, HEAD_DIM)
    q_pos = i * qb + lax.broadcasted_iota(I32, (qb, 1), 0)

    s = _dot_nt(q, kc_ref[...]).reshape(GROUP_NSA, qb, n_cmp_pad)
    n = lax.broadcasted_iota(I32, (qb, n_cmp_pad), 1)
    mask_c = ((n * CMP_STRIDE + CMP_LEN - 1 <= q_pos) & (n < n_cmp))[None]
    p_c = _masked_softmax(s, mask_c)
    o_c = _dot(p_c.reshape(GROUP_NSA * qb, n_cmp_pad).astype(BF16), vc_ref[...])

    imp = _importance(jnp.sum(p_c, axis=0), n_cmp_pad, n_sel_pad)
    score = _block_scores(imp, q_pos, n_sel)
    blk = lax.broadcasted_iota(I32, (qb, n_sel_pad), 1).astype(F32)
    sel = jnp.zeros((qb, n_sel_pad), F32)
    for val, idx in _topk_rounds(score, topk):
        sel = jnp.where((blk == idx) & (val > 0.5 * NEG_INF), 1.0, sel)
    sel_b = sel.astype(BF16)

    bpk = kb // SEL_BLOCK
    nkb = (i * qb + qb + kb - 1) // kb

    def sel_body(j, carry):
        m, l, acc = carry
        start = pl.multiple_of(j * kb, kb)
        k = ks_ref[pl.ds(start, kb), :]
        v = vs_ref[pl.ds(start, kb), :]
        s = _dot_nt(q, k).reshape(GROUP_NSA, qb, kb)
        eb = lax.broadcasted_iota(I32, (n_sel_pad, kb), 0)
        ek = lax.broadcasted_iota(I32, (n_sel_pad, kb), 1)
        expand = jnp.where(eb == j * bpk + ek // SEL_BLOCK, 1.0, 0.0).astype(BF16)
        picked = _dot(sel_b, expand)
        k_pos = start + lax.broadcasted_iota(I32, (qb, kb), 1)
        ok = ((picked > 0.5) & (k_pos <= q_pos))[None]
        s = jnp.where(ok, s, NEG_INF)
        m_new = jnp.maximum(m, jnp.max(s, axis=-1, keepdims=True))
        e = jnp.where(ok, jnp.exp(s - m_new), 0.0)
        alpha = jnp.exp(m - m_new)
        l = alpha * l + jnp.sum(e, axis=-1, keepdims=True)
        pv = _dot(e.reshape(GROUP_NSA * qb, kb).astype(BF16), v).reshape(GROUP_NSA, qb, HEAD_DIM)
        return m_new, l, alpha * acc + pv

    m0 = jnp.full((GROUP_NSA, qb, 1), NEG_INF, F32)
    l0 = jnp.zeros((GROUP_NSA, qb, 1), F32)
    a0 = jnp.zeros((GROUP_NSA, qb, HEAD_DIM), F32)
    _, l_s, acc_s = lax.fori_loop(0, nkb, sel_body, (m0, l0, a0))
    o_s = acc_s / jnp.maximum(l_s, TINY)

    wlen = WINDOW + qb
    wstart = pl.multiple_of(jnp.maximum(i * qb - WINDOW, 0), qb)
    kw = kw_ref[pl.ds(wstart, wlen), :]
    vw = vw_ref[pl.ds(wstart, wlen), :]
    s = _dot_nt(q, kw).reshape(GROUP_NSA, qb, wlen)
    dist = q_pos - (wstart + lax.broadcasted_iota(I32, (qb, wlen), 1))
    p_w = _masked_softmax(s, ((dist >= 0) & (dist < WINDOW))[None])
    o_w = _dot(p_w.reshape(GROUP_NSA * qb, wlen).astype(BF16), vw).reshape(GROUP_NSA, qb, HEAD_DIM)

    o_c = o_c.reshape(GROUP_NSA, qb, HEAD_DIM)
    gates = gate_ref[...]
    for r in range(GROUP_NSA):
        o = (gates[:, 3 * r:3 * r + 1] * o_c[r] + gates[:, 3 * r + 1:3 * r + 2] * o_s[r]
             + gates[:, 3 * r + 2:3 * r + 3] * o_w[r])
        o_ref[r] = o.astype(o_ref.dtype)


def _nsa_prompt_call(qa, kcmp, vcmp, kslc, vslc, kwin, vwin, gates, n_cmp):
    t = qa.shape[1]
    qb, kb = 128, 256
    assert t % kb == 0 and t >= WINDOW + qb
    n_sel = t // SEL_BLOCK
    n_cmp_pad = kcmp.shape[1]
    kern = functools.partial(_nsa_prompt_kernel, qb=qb, kb=kb, n_cmp=n_cmp, n_sel=n_sel, topk=min(TOP_N, n_sel))
    whole = lambda rows: pl.BlockSpec((None, rows, HEAD_DIM), lambda g, i: (g, 0, 0))
    return pl.pallas_call(
        kern,
        grid=(N_KV_NSA, t // qb),
        in_specs=[pl.BlockSpec((GROUP_NSA, qb, HEAD_DIM), lambda g, i: (g, i, 0)),
                  whole(n_cmp_pad), whole(n_cmp_pad), whole(t), whole(t), whole(t), whole(t),
                  pl.BlockSpec((qb, LANES), lambda g, i: (i, g))],
        out_specs=pl.BlockSpec((GROUP_NSA, qb, HEAD_DIM), lambda g, i: (g, i, 0)),
        out_shape=jax.ShapeDtypeStruct((N_HEADS_NSA, t, HEAD_DIM), BF16),
        compiler_params=_cparams(("arbitrary", "arbitrary")),
        name="nsa_prompt",
    )(qa, kcmp, vcmp, kslc, vslc, kwin, vwin, gates)


def _sb_block(z, mask, carry, v):
    kb = z.shape[1]
    l1 = jnp.where(mask, -_softplus(z), 0.0)
    sp = lax.broadcasted_iota(I32, (kb, kb), 0)
    sc = lax.broadcasted_iota(I32, (kb, kb), 1)
    later = jnp.where(sp > sc, 1.0, 0.0).astype(BF16)
    hi = l1.astype(BF16)
    lo = (l1 - hi.astype(F32)).astype(BF16)
    after = _dot(hi, later) + _dot(lo, later) + carry
    a = jnp.where(mask, jnp.exp(z + l1 + after), 0.0)
    return _dot(a.astype(BF16), v), carry + jnp.sum(l1, axis=-1, keepdims=True)


def _sb_prompt_kernel(q_ref, k_ref, v_ref, o_ref, *, qb, kb):
    i = pl.program_id(1)
    q = q_ref[...]
    q_pos = i * qb + lax.broadcasted_iota(I32, (qb, 1), 0)

    def cond(st):
        j, _, _, cmax = st
        return (j >= 0) & (cmax > SB_EXIT)

    def body(st):
        j, carry, acc, _ = st
        start = pl.multiple_of(j * kb, kb)
        k = k_ref[pl.ds(start, kb), :]
        v = v_ref[pl.ds(start, kb), :]
        z = _dot_nt(q, k)
        mask = (start + lax.broadcasted_iota(I32, (qb, kb), 1)) < q_pos
        pv, carry = _sb_block(z, mask, carry, v)
        return j - 1, carry, acc + pv, jnp.max(carry)

    j0 = (i * qb + qb - 1) // kb
    st = (j0, jnp.zeros((qb, 1), F32), jnp.zeros((qb, HEAD_DIM), F32), jnp.float32(0.0))
    _, _, acc, _ = lax.while_loop(cond, body, st)
    o_ref[...] = acc.astype(o_ref.dtype)


def _sb_prompt_call(qb_hm, ksb, vsb):
    t = qb_hm.shape[1]
    qb, kb = 128, 256
    assert t % kb == 0
    kern = functools.partial(_sb_prompt_kernel, qb=qb, kb=kb)
    whole = pl.BlockSpec((None, t, HEAD_DIM), lambda h, i: (h, 0, 0))
    blk = pl.BlockSpec((None, qb, HEAD_DIM), lambda h, i: (h, i, 0))
    return pl.pallas_call(
        kern,
        grid=(N_HEADS_SB, t // qb),
        in_specs=[blk, whole, whole],
        out_specs=blk,
        out_shape=jax.ShapeDtypeStruct((N_HEADS_SB, t, HEAD_DIM), BF16),
        compiler_params=_cparams(("arbitrary", "arbitrary")),
        name="sb_prompt",
    )(qb_hm, ksb, vsb)


def _merge_kernel(oa_ref, ob_ref, mg_ref, wa_ref, wb_ref, wo_ref, x_ref, gate_ref, ng1_ref, ng2_ref,
                  shift_ref, scale_ref, x1_ref, h2_ref):
    ya = _dot(oa_ref[0], wa_ref[0])
    yb = _dot(ob_ref[0], wb_ref[0])
    for h in range(1, N_HEADS_NSA):
        ya = ya + _dot(oa_ref[h], wa_ref[h])
        yb = yb + _dot(ob_ref[h], wb_ref[h])
    y = mg_ref[:, 0:D_MODEL].astype(F32) * ya + mg_ref[:, D_MODEL:2 * D_MODEL].astype(F32) * yb
    o = _dot(y.astype(BF16), wo_ref[...])
    x1 = x_ref[...] + gate_ref[...] * (_rms(o) * ng1_ref[...])
    x1_ref[...] = x1
    h2 = (_rms(x1) * ng2_ref[...]) * (1.0 + scale_ref[...]) + shift_ref[...]
    h2_ref[...] = h2.astype(BF16)


def _merge_call(oa, ob, mg, wa, wb, wo, x, gate, ng1, ng2, shift, scale, tm):
    rows = x.shape[0]
    per_row = gate.shape[0] != 1
    mod_spec = (pl.BlockSpec((tm, D_MODEL), lambda i: (i, 0)) if per_row
                else pl.BlockSpec((1, D_MODEL), lambda i: (0, 0)))
    vec = pl.BlockSpec((1, D_MODEL), lambda i: (0, 0))
    hm = pl.BlockSpec((8, tm, HEAD_DIM), lambda i: (0, i, 0))
    wspec = pl.BlockSpec((8, HEAD_DIM, D_MODEL), lambda i: (0, 0, 0))
    row = lambda n: pl.BlockSpec((tm, n), lambda i: (i, 0))
    return pl.pallas_call(
        _merge_kernel,
        grid=(rows // tm,),
        in_specs=[hm, hm, row(2 * D_MODEL), wspec, wspec,
                  pl.BlockSpec((D_MODEL, D_MODEL), lambda i: (0, 0)),
                  row(D_MODEL), mod_spec, vec, vec, mod_spec, mod_spec],
        out_specs=(row(D_MODEL), row(D_MODEL)),
        out_shape=(jax.ShapeDtypeStruct((rows, D_MODEL), F32), jax.ShapeDtypeStruct((rows, D_MODEL), BF16)),
        compiler_params=_cparams(("arbitrary",)),
        name="merge_out",
    )(oa, ob, mg, wa, wb, wo, x, gate, ng1, ng2, shift, scale)


def _ffn_kernel(h_ref, wg_ref, wu_ref, wd_ref, x1_ref, gate_ref, ng_ref, o_ref, acc_ref):
    k = pl.program_id(1)

    @pl.when(k == 0)
    def _():
        acc_ref[...] = jnp.zeros_like(acc_ref)

    h = h_ref[...]
    g = _dot(h, wg_ref[...])
    u = _dot(h, wu_ref[...])
    act = (g * jax.nn.sigmoid(g)) * u
    acc_ref[...] += _dot(act.astype(BF16), wd_ref[...])

    @pl.when(k == pl.num_programs(1) - 1)
    def _():
        o_ref[...] = x1_ref[...] + gate_ref[...] * (_rms(acc_ref[...]) * ng_ref[...])


def _ffn_call(h2, w_up, w_down, x1, gate, ng, tm):
    rows = h2.shape[0]
    d_ff = w_down.shape[0]
    tk = 256
    nk = d_ff // tk
    assert d_ff % tk == 0
    per_row = gate.shape[0] != 1
    mod_spec = (pl.BlockSpec((tm, D_MODEL), lambda i, k: (i, 0)) if per_row
                else pl.BlockSpec((1, D_MODEL), lambda i, k: (0, 0)))
    row = pl.BlockSpec((tm, D_MODEL), lambda i, k: (i, 0))
    return pl.pallas_call(
        _ffn_kernel,
        grid=(rows // tm, nk),
        in_specs=[row,
                  pl.BlockSpec((D_MODEL, tk), lambda i, k: (0, k)),
                  pl.BlockSpec((D_MODEL, tk), lambda i, k: (0, nk + k)),
                  pl.BlockSpec((tk, D_MODEL), lambda i, k: (k, 0)),
                  row, mod_spec,
                  pl.BlockSpec((1, D_MODEL), lambda i, k: (0, 0))],
        out_specs=row,
        out_shape=jax.ShapeDtypeStruct((rows, D_MODEL), F32),
        scratch_shapes=[pltpu.VMEM((tm, D_MODEL), F32)],
        compiler_params=_cparams(("arbitrary", "arbitrary")),
        name="ffn",
    )(h2, w_up, w_up, w_down, x1, gate, ng)


def _nsa_s1_kernel(q_ref, kc_ref, vc_ref, oc_ref, idx_ref, *, nt, past_len, n_cmp, n_sel, topk):
    n_cmp_pad = kc_ref.shape[1]
    n_sel_pad = ((n_sel + LANES - 1) // LANES) * LANES
    rows = GROUP_NSA * nt
    q_pos = past_len + lax.broadcasted_iota(I32, (rows, 1), 0) % nt
    n = lax.broadcasted_iota(I32, (rows, n_cmp_pad), 1)
    mask = (n * CMP_STRIDE + CMP_LEN - 1 <= q_pos) & (n < n_cmp)
    probs = []
    for g in range(N_KV_NSA):
        q = q_ref[g * rows:(g + 1) * rows, :]
        p = _masked_softmax(_dot_nt(q, kc_ref[g]), mask)
        oc_ref[g * rows:(g + 1) * rows, :] = _dot(p.astype(BF16), vc_ref[g])
        probs.append(p)
    p_all = jnp.concatenate(probs, axis=0)
    ro = lax.broadcasted_iota(I32, (N_KV_NSA * nt, N_KV_NSA * rows), 0)
    ri = lax.broadcasted_iota(I32, (N_KV_NSA * nt, N_KV_NSA * rows), 1)
    pick = jnp.where((ro // nt == ri // rows) & (ro % nt == ri % nt), 1.0, 0.0).astype(BF16)
    p1, p2, p3 = _split3(p_all)
    p_sum = _dot(pick, p1) + _dot(pick, p2) + _dot(pick, p3)
    imp = _importance(p_sum, n_cmp_pad, n_sel_pad)
    t_pos = past_len + lax.broadcasted_iota(I32, (N_KV_NSA * nt, 1), 0) % nt
    score = _block_scores(imp, t_pos, n_sel)
    lane = lax.broadcasted_iota(I32, (N_KV_NSA * nt, LANES), 1)
    out = jnp.full((N_KV_NSA * nt, LANES), -1.0, F32)
    for r, (val, idx) in enumerate(_topk_rounds(score, topk)):
        out = jnp.where(lane == r, jnp.where(val > 0.5 * NEG_INF, idx, -1.0), out)
    idx_ref[...] = out.astype(I32)


def _nsa_s1_call(q1, kcmp, vcmp, nt, past_len, n_cmp, n_sel):
    nb = q1.shape[0]
    n_half = kcmp.shape[2]
    kern = functools.partial(_nsa_s1_kernel, nt=nt, past_len=past_len, n_cmp=n_cmp, n_sel=n_sel,
                             topk=min(TOP_N, n_sel))
    rows = N_HEADS_NSA * nt
    per_b = lambda *shape: pl.BlockSpec((None,) + shape, lambda b: (b,) + (0,) * len(shape))
    return pl.pallas_call(
        kern,
        grid=(nb,),
        in_specs=[per_b(rows, HEAD_DIM), per_b(N_KV_NSA, n_half, HEAD_DIM), per_b(N_KV_NSA, n_half, HEAD_DIM)],
        out_specs=(per_b(rows, HEAD_DIM), per_b(N_KV_NSA * nt, LANES)),
        out_shape=(jax.ShapeDtypeStruct((nb, rows, HEAD_DIM), F32),
                   jax.ShapeDtypeStruct((nb, N_KV_NSA * nt, LANES), I32)),
        compiler_params=_cparams(("arbitrary",)),
        name="nsa_sample_topk",
    )(q1, kcmp, vcmp)


def _nsa_s2_kernel(idx_s, pt_s, q_ref, idxv_ref, oc_ref, gate_ref, newkv_ref, wkv_ref, wnew_ref, blocks_hbm,
                   o_ref, kvbuf, sem, os_scr, *, nt, n_pages, n_past_blocks, topk):
    b = pl.program_id(0)
    ngt = N_KV_NSA * nt
    bpp = PAGE // SEL_BLOCK

    def slot_copy(j, gt, s):
        page = pt_s[b * n_pages + j // bpp]
        return pltpu.make_async_copy(blocks_hbm.at[page * bpp + j % bpp, gt // nt],
                                     kvbuf.at[gt, pl.ds(s * SEL_BLOCK, SEL_BLOCK)], sem.at[0])

    def issue(n, _):
        gt, s = n // topk, n % topk
        j = idx_s[(b * ngt + gt) * topk + s]
        valid = (j >= 0) & (j < n_past_blocks)

        @pl.when(valid)
        def _():
            slot_copy(j, gt, s).start()

        @pl.when(jnp.logical_not(valid))
        def _():
            kvbuf[gt, pl.ds(s * SEL_BLOCK, SEL_BLOCK), :] = jnp.zeros((SEL_BLOCK, LANES), BF16)
        return 0

    def drain(n, _):
        gt, s = n // topk, n % topk
        j = idx_s[(b * ngt + gt) * topk + s]

        @pl.when((j >= 0) & (j < n_past_blocks))
        def _():
            slot_copy(j, gt, s).wait()
        return 0

    lax.fori_loop(0, ngt * topk, issue, 0)

    rows = GROUP_NSA * nt
    t_row = lax.broadcasted_iota(I32, (rows, 1), 0) // GROUP_NSA
    nw = wkv_ref.shape[1]
    nn = newkv_ref.shape[1]
    new_ok = lax.broadcasted_iota(I32, (rows, nn), 1) <= t_row
    win_ok = lax.broadcasted_iota(I32, (rows, nw), 1) > t_row
    o_w = []
    for g in range(N_KV_NSA):
        q = q_ref[g * rows:(g + 1) * rows, :]
        s_old = jnp.where(win_ok, _dot_nt(q, wkv_ref[g]), NEG_INF)
        s_new = jnp.where(new_ok, _dot_nt(q, wnew_ref[g]), NEG_INF)
        m = jnp.maximum(jnp.max(s_old, axis=-1, keepdims=True), jnp.max(s_new, axis=-1, keepdims=True))
        e_old = jnp.where(win_ok, jnp.exp(s_old - m), 0.0)
        e_new = jnp.where(new_ok, jnp.exp(s_new - m), 0.0)
        l = jnp.sum(e_old, axis=-1, keepdims=True) + jnp.sum(e_new, axis=-1, keepdims=True)
        pv = _dot(e_old.astype(BF16), wkv_ref[g]) + _dot(e_new.astype(BF16), wnew_ref[g])
        o_w.append(pv[:, HEAD_DIM:] / jnp.maximum(l, TINY))
    o_w = jnp.concatenate(o_w, axis=0)

    lax.fori_loop(0, ngt * topk, drain, 0)

    idxv = idxv_ref[...]
    slot_valid = jnp.where((idxv >= 0) & (idxv < n_past_blocks), 1.0, 0.0).astype(BF16)
    es = lax.broadcasted_iota(I32, (LANES, topk * SEL_BLOCK), 0)
    ek = lax.broadcasted_iota(I32, (LANES, topk * SEL_BLOCK), 1)
    expand = jnp.where(es == ek // SEL_BLOCK, 1.0, 0.0).astype(BF16)
    key_valid = _dot(slot_valid, expand) > 0.5
    for gt in range(ngt):
        g, t = gt // nt, gt % nt
        q = q_ref[gt * GROUP_NSA:(gt + 1) * GROUP_NSA, :]
        kv = kvbuf[gt]
        kvn = newkv_ref[g]
        ok_old = key_valid[gt:gt + 1, :]
        ok_new = lax.broadcasted_iota(I32, (GROUP_NSA, nn), 1) <= t
        s_old = jnp.where(ok_old, _dot_nt(q, kv), NEG_INF)
        s_new = jnp.where(ok_new, _dot_nt(q, kvn), NEG_INF)
        m = jnp.maximum(jnp.max(s_old, axis=-1, keepdims=True), jnp.max(s_new, axis=-1, keepdims=True))
        e_old = jnp.where(ok_old, jnp.exp(s_old - m), 0.0)
        e_new = jnp.where(ok_new, jnp.exp(s_new - m), 0.0)
        l = jnp.sum(e_old, axis=-1, keepdims=True) + jnp.sum(e_new, axis=-1, keepdims=True)
        pv = _dot(e_old.astype(BF16), kv) + _dot(e_new.astype(BF16), kvn)
        os_scr[gt * GROUP_NSA:(gt + 1) * GROUP_NSA, :] = pv[:, HEAD_DIM:] / jnp.maximum(l, TINY)

    gates = gate_ref[...]
    o_ref[...] = gates[:, 0:1] * oc_ref[...] + gates[:, 1:2] * os_scr[...] + gates[:, 2:3] * o_w


def _nsa_s2_call(idx, page_table, q2, oc, gates, newkv, wkv, wnew, blocks, nt, past_len):
    nb, n_pages = page_table.shape
    topk = TOP_N
    ngt = N_KV_NSA * nt
    rows = N_HEADS_NSA * nt
    kern = functools.partial(_nsa_s2_kernel, nt=nt, n_pages=n_pages, n_past_blocks=past_len // SEL_BLOCK, topk=topk)
    per_b = lambda *shape: pl.BlockSpec((None,) + shape, lambda b, i_s, p_s: (b,) + (0,) * len(shape))
    grid_spec = pltpu.PrefetchScalarGridSpec(
        num_scalar_prefetch=2,
        grid=(nb,),
        in_specs=[per_b(rows, LANES), per_b(ngt, LANES), per_b(rows, HEAD_DIM), per_b(rows, LANES),
                  per_b(N_KV_NSA, newkv.shape[2], LANES), per_b(N_KV_NSA, wkv.shape[2], LANES),
                  per_b(N_KV_NSA, wnew.shape[2], LANES), pl.BlockSpec(memory_space=pl.ANY)],
        out_specs=per_b(rows, HEAD_DIM),
        scratch_shapes=[pltpu.VMEM((ngt, topk * SEL_BLOCK, LANES), BF16),
                        pltpu.SemaphoreType.DMA((1,)),
                        pltpu.VMEM((rows, HEAD_DIM), F32)],
    )
    return pl.pallas_call(
        kern, grid_spec=grid_spec,
        out_shape=jax.ShapeDtypeStruct((nb, rows, HEAD_DIM), F32),
        compiler_params=_cparams(("arbitrary",)),
        name="nsa_sample_gather",
    )(idx[:, :, :topk].reshape(-1), page_table.reshape(-1), q2, idx, oc, gates, newkv, wkv, wnew, blocks)


def _sb_sample_kernel(pt_s, qbd_ref, newpage_ref, pages_hbm, o_ref, buf, sem, *, nt, n_pages):
    b = pl.program_id(0)
    rows = nt * N_HEADS_SB
    width = N_HEADS_SB * HEAD_DIM
    qbd = qbd_ref[...]
    own_head = (lax.broadcasted_iota(I32, (rows, width), 0) % N_HEADS_SB
                == lax.broadcasted_iota(I32, (rows, width), 1) // HEAD_DIM)

    def step(kv, mask, carry, acc):
        z = _dot_nt(qbd, kv[:, :width])
        pv, carry = _sb_block(z, mask, carry, kv[:, width:])
        return carry, acc + jnp.where(own_head, pv, 0.0)

    def copy(pg, slot):
        return pltpu.make_async_copy(pages_hbm.at[pt_s[b * n_pages + pg]], buf.at[slot], sem.at[slot])

    copy(n_pages - 1, 0).start()
    t_row = lax.broadcasted_iota(I32, (rows, PAGE), 0) // N_HEADS_SB
    new_mask = lax.broadcasted_iota(I32, (rows, PAGE), 1) < t_row
    carry, acc = step(newpage_ref[...], new_mask, jnp.zeros((rows, 1), F32), jnp.zeros((rows, width), F32))
    all_keys = jnp.full((rows, PAGE), True)

    def cond(st):
        pg, _, _, _, cmax = st
        return (pg >= 0) & (cmax > SB_EXIT)

    def body(st):
        pg, slot, carry, acc, _ = st
        copy(pg, slot).wait()

        @pl.when(pg > 0)
        def _():
            copy(pg - 1, 1 - slot).start()

        carry, acc = step(buf[slot], all_keys, carry, acc)
        return pg - 1, 1 - slot, carry, acc, jnp.max(carry)

    pg, slot, _, acc, _ = lax.while_loop(cond, body, (jnp.int32(n_pages - 1), jnp.int32(0), carry, acc, jnp.max(carry)))

    @pl.when(pg >= 0)
    def _():
        copy(pg, slot).wait()

    o_ref[...] = jnp.sum(acc.reshape(nt, N_HEADS_SB, width), axis=1)


def _sb_sample_call(page_table, qbd, newpage, pages, nt):
    nb, n_pages = page_table.shape
    width = N_HEADS_SB * HEAD_DIM
    kern = functools.partial(_sb_sample_kernel, nt=nt, n_pages=n_pages)
    per_b = lambda *shape: pl.BlockSpec((None,) + shape, lambda b, p_s: (b,) + (0,) * len(shape))
    grid_spec = pltpu.PrefetchScalarGridSpec(
        num_scalar_prefetch=1,
        grid=(nb,),
        in_specs=[per_b(nt * N_HEADS_SB, width), per_b(PAGE, 2 * width), pl.BlockSpec(memory_space=pl.ANY)],
        out_specs=per_b(nt, width),
        scratch_shapes=[pltpu.VMEM((2, PAGE, 2 * width), BF16), pltpu.SemaphoreType.DMA((2,))],
    )
    return pl.pallas_call(
        kern, grid_spec=grid_spec,
        out_shape=jax.ShapeDtypeStruct((nb, nt, width), F32),
        compiler_params=_cparams(("arbitrary",)),
        name="sb_sample",
    )(page_table.reshape(-1), qbd, newpage, pages)


def _rope_tables(pos):
    half = ROPE_DIM // 2
    inv_freq = ROPE_THETA ** (-jnp.arange(half, dtype=F32) / half)
    ang = pos.astype(F32)[:, None] * inv_freq[None, :]
    cos, sin = jnp.cos(ang), jnp.sin(ang)
    one = jnp.ones((pos.shape[0], HEAD_DIM - ROPE_DIM), F32)
    c64 = jnp.concatenate([cos, cos, one], axis=1)
    s64 = jnp.concatenate([-sin, sin, jnp.zeros_like(one)], axis=1)
    return jnp.tile(c64, (1, LANES // HEAD_DIM)), jnp.tile(s64, (1, LANES // HEAD_DIM))


def _pad_w_in(w_in):
    c_ga = 512 + 768
    n_ga = 3 * N_HEADS_NSA
    ga = w_in[:, c_ga:c_ga + n_ga].reshape(D_MODEL, N_KV_NSA, n_ga // N_KV_NSA)
    ga = jnp.pad(ga, ((0, 0), (0, 0), (0, LANES - n_ga // N_KV_NSA))).reshape(D_MODEL, N_KV_NSA * LANES)
    return jnp.concatenate([w_in[:, :c_ga], ga, w_in[:, c_ga + n_ga:]], axis=1).astype(BF16)


def _pad_axis(x, axis, size):
    pad = [(0, 0)] * x.ndim
    pad[axis] = (0, size - x.shape[axis])
    return jnp.pad(x, pad)


def kernel(x_prompt, x_sample, cache_nsa_kv, cache_sb_kv, state_win_kv, page_table, c_prompt, c_sample, ada_w, ada_b, norm_g, w_in, cmp_pe, cmp_w1, cmp_b1, cmp_w2, cmp_b2, w_o_nsa, w_o_sb, w_out, ffn_w_up, ffn_w_down):
    assert ada_w.shape[0] == 1 and x_prompt.shape[0] == 1
    t = x_prompt.shape[1]
    nb, nt = x_sample.shape[:2]
    n_pages = page_table.shape[1]
    past_len = n_pages * PAGE
    n_pool = cache_nsa_kv.shape[1]
    assert nt <= 8 and state_win_kv.shape[2] == WINDOW and t >= WINDOW

    w = _pad_w_in(w_in[0])
    cw = _compress_weights(cmp_pe[0], cmp_w1[0], cmp_b1[0], cmp_w2[0], cmp_b2[0])
    wa = w_o_nsa[0].astype(BF16).reshape(N_HEADS_NSA, HEAD_DIM, D_MODEL)
    wb = w_o_sb[0].astype(BF16).reshape(N_HEADS_SB, HEAD_DIM, D_MODEL)
    wo = w_out[0].astype(BF16)
    wup = ffn_w_up[0].astype(BF16)
    wdn = ffn_w_down[0].astype(BF16)
    ng = norm_g[0]

    c_all = _pad_axis(jnp.concatenate([c_prompt, c_sample], axis=0), 0, ((1 + nb + 7) // 8) * 8)
    mod = _mod_call(c_all, ada_w[0], ada_b[0][None])
    modp = [mod[0:1, k * D_MODEL:(k + 1) * D_MODEL] for k in range(6)]
    mods = [jnp.repeat(mod[1:1 + nb, k * D_MODEL:(k + 1) * D_MODEL], nt, axis=0) for k in range(6)]

    xp = x_prompt[0]
    cos, sin = _rope_tables(jnp.arange(t))
    (qa, nsa_rows, win_rows, kslc, vslc, kwin, vwin, gates, qb, sb_rows, ksb, vsb, mg) = _proj_call(
        xp, modp[0], modp[1], ng[0:1], w, cos, sin, tm=256)
    ident = jnp.arange(t // PAGE, dtype=I32).reshape(1, -1)
    kcmp, vcmp = _compress_call(ident, nsa_rows.reshape(t // PAGE, PAGE, 512), *cw)
    oa = _nsa_prompt_call(qa, kcmp[0], vcmp[0], kslc, vslc, kwin, vwin, gates, n_cmp=t // CMP_STRIDE - 1)
    ob = _sb_prompt_call(qb, ksb, vsb)
    x1, h2 = _merge_call(oa, ob, mg, wa, wb, wo, xp, modp[2], ng[1:2], ng[2:3], modp[3], modp[4], tm=256)
    y_prompt = _ffn_call(h2, wup, wdn, x1, modp[5], ng[3:4], tm=512)

    rs = nb * nt
    xs = x_sample.reshape(rs, D_MODEL)
    cos_s, sin_s = _rope_tables(past_len + jnp.tile(jnp.arange(nt), nb))
    (qa_s, nsa_new, win_new, kslc_s, vslc_s, kwin_s, vwin_s, gates_s, qb_s, sb_new, _, _, mg_s) = _proj_call(
        xs, mods[0], mods[1], ng[0:1], w, cos_s, sin_s, tm=rs)

    cn = cache_nsa_kv[0]
    cmp_pages = cn[:, :, 0:2].reshape(n_pool, PAGE, 4 * HEAD_DIM)
    bpp = PAGE // SEL_BLOCK
    blocks = cn[:, :, 2:4].reshape(n_pool, bpp, SEL_BLOCK, 2, N_KV_NSA, HEAD_DIM)
    blocks = blocks.transpose(0, 1, 4, 2, 3, 5).reshape(n_pool * bpp, N_KV_NSA, SEL_BLOCK, 2 * HEAD_DIM).astype(BF16)
    kcmp_s, vcmp_s = _compress_call(page_table, cmp_pages, *cw)
    n_cmp = past_len // CMP_STRIDE - 1
    n_sel = past_len // SEL_BLOCK + 1

    qa5 = qa_s.reshape(N_KV_NSA, GROUP_NSA, nb, nt, HEAD_DIM)
    q1 = qa5.transpose(2, 0, 1, 3, 4).reshape(nb, N_HEADS_NSA * nt, HEAD_DIM)
    oc, idx = _nsa_s1_call(q1, kcmp_s, vcmp_s, nt, past_len, n_cmp, n_sel)
    q2 = qa5.transpose(2, 0, 3, 1, 4).reshape(nb, N_HEADS_NSA * nt, HEAD_DIM)
    q2 = _pad_axis(q2, 2, LANES)
    oc2 = oc.reshape(nb, N_KV_NSA, GROUP_NSA, nt, HEAD_DIM).transpose(0, 1, 3, 2, 4).reshape(nb, -1, HEAD_DIM)
    g2 = gates_s.reshape(nb, nt, N_KV_NSA, LANES)[..., :3 * GROUP_NSA].reshape(nb, nt, N_KV_NSA, GROUP_NSA, 3)
    g2 = _pad_axis(g2.transpose(0, 2, 1, 3, 4).reshape(nb, N_HEADS_NSA * nt, 3), 2, LANES)

    def per_batch_kv(k, v):
        kv = jnp.concatenate([k, v], axis=-1).reshape(N_KV_NSA, nb, nt, LANES).transpose(1, 0, 2, 3)
        return _pad_axis(kv, 2, 8)

    newkv = per_batch_kv(kslc_s, vslc_s)
    wnew = per_batch_kv(kwin_s, vwin_s)
    wkv = state_win_kv[0].transpose(0, 3, 1, 2, 4).reshape(nb, N_KV_NSA, WINDOW, LANES).astype(BF16)
    oa_s = _nsa_s2_call(idx, page_table, q2, oc2, g2, newkv, wkv, wnew, blocks, nt, past_len)
    oa_s = oa_s.reshape(nb, N_KV_NSA, nt, GROUP_NSA, HEAD_DIM).transpose(1, 3, 0, 2, 4)
    oa_s = oa_s.reshape(N_HEADS_NSA, rs, HEAD_DIM).astype(BF16)

    qb4 = qb_s.reshape(N_HEADS_SB, nb, nt, HEAD_DIM).transpose(1, 2, 0, 3)
    eye = jnp.eye(N_HEADS_SB, dtype=BF16)
    qbd = (qb4[:, :, :, None, :] * eye[None, None, :, :, None]).reshape(nb, nt * N_HEADS_SB, N_HEADS_SB * HEAD_DIM)
    newpage = _pad_axis(sb_new.reshape(nb, nt, 2 * N_HEADS_SB * HEAD_DIM), 1, PAGE).astype(BF16)
    sb_pages = cache_sb_kv[0].reshape(n_pool, PAGE, 2 * N_HEADS_SB * HEAD_DIM).astype(BF16)
    ob_s = _sb_sample_call(page_table, qbd, newpage, sb_pages, nt)
    ob_s = ob_s.reshape(rs, N_HEADS_SB, HEAD_DIM).transpose(1, 0, 2).astype(BF16)

    x1s, h2s = _merge_call(oa_s, ob_s, mg_s, wa, wb, wo, xs, mods[2], ng[1:2], ng[2:3], mods[3], mods[4], tm=rs)
    y_sample = _ffn_call(h2s, wup, wdn, x1s, mods[5], ng[3:4], tm=rs)

    win_new5 = win_new.reshape(nb, nt, 2, N_KV_NSA, HEAD_DIM)
    win_kv_sample = jnp.concatenate([state_win_kv[0], win_new5], axis=1)[:, nt:][:, -WINDOW:]
    return (y_prompt[None],
            y_sample.reshape(nb, nt, D_MODEL),
            nsa_rows.reshape(1, 1, t, 4, N_KV_NSA, HEAD_DIM),
            sb_rows.reshape(1, 1, t, 2, N_HEADS_SB, HEAD_DIM),
            win_rows[t - WINDOW:].reshape(1, 1, WINDOW, 2, N_KV_NSA, HEAD_DIM),
            nsa_new.reshape(1, nb, nt, 4, N_KV_NSA, HEAD_DIM),
            sb_new.reshape(1, nb, nt, 2, N_HEADS_SB, HEAD_DIM),
            win_kv_sample[None])
```

```python
import functools

import jax
import jax.numpy as jnp
from jax import lax
from jax.experimental import pallas as pl
from jax.experimental.pallas import tpu as pltpu

F32 = jnp.float32
BF16 = jnp.bfloat16
I32 = jnp.int32

D_MODEL = 1024
HEAD_DIM = 64
N_HEADS_NSA = 8
N_KV_NSA = 2
GROUP_NSA = N_HEADS_NSA // N_KV_NSA
N_HEADS_SB = 8
ROPE_DIM = HEAD_DIM // 4
ROPE_THETA = 500000.0
CMP_STRIDE = 16
CMP_LEN = 2 * CMP_STRIDE
CMP_HIDDEN = 4 * HEAD_DIM
SEL_BLOCK = 64
SEL_RATIO = SEL_BLOCK // CMP_STRIDE
TOP_N = 16
WINDOW = 512
PAGE = 128
RMS_EPS = 1e-6
NEG_INF = -1e30
BIG = 1e30
TINY = 1e-30
ATTN_SCALE = HEAD_DIM ** -0.5

LANES = 128
SUBLANES = 8
VMEM_LIMIT = 56 * 1024 * 1024
SB_EXIT = -104.0

C_QA = 0
C_KVA = 512
C_GA = 1280
C_QB = 1536
C_KVB = 2048
C_MG = 3072
C_END = 5120

Q_NSA = N_HEADS_NSA * HEAD_DIM
Q_SB = N_HEADS_SB * HEAD_DIM
QG = GROUP_NSA * HEAD_DIM


def _cparams(sem):
    return pltpu.CompilerParams(dimension_semantics=sem, vmem_limit_bytes=VMEM_LIMIT)


def _rms(x):
    return x * lax.rsqrt(jnp.mean(x * x, axis=-1, keepdims=True) + RMS_EPS)


def _dot(a, b):
    return jnp.dot(a, b, preferred_element_type=F32)


def _dot_nt(a, b):
    return lax.dot_general(a, b, (((1,), (1,)), ((), ())), preferred_element_type=F32)


def _split3(x):
    p1 = x.astype(BF16)
    r1 = x - p1.astype(F32)
    p2 = r1.astype(BF16)
    p3 = (r1 - p2.astype(F32)).astype(BF16)
    return p1, p2, p3


def _softplus(z):
    return jnp.maximum(z, 0.0) + jnp.log(1.0 + jnp.exp(-jnp.abs(z)))


def _mod_kernel(c_ref, w_ref, b_ref, o_ref):
    c = c_ref[...]
    s = c * jax.nn.sigmoid(c)
    o_ref[...] = _dot(s.astype(BF16), w_ref[...].astype(BF16)) + b_ref[...]


def _mod_call(c, w, b):
    rows, n = c.shape[0], w.shape[1]
    tn = 1536
    return pl.pallas_call(
        _mod_kernel,
        grid=(n // tn,),
        in_specs=[pl.BlockSpec((rows, D_MODEL), lambda j: (0, 0)),
                  pl.BlockSpec((D_MODEL, tn), lambda j: (0, j)),
                  pl.BlockSpec((1, tn), lambda j: (0, j))],
        out_specs=pl.BlockSpec((rows, tn), lambda j: (0, j)),
        out_shape=jax.ShapeDtypeStruct((rows, n), F32),
        compiler_params=_cparams(("arbitrary",)),
        name="adaln_mod",
    )(c, w, b)


def _rope(y, cos, sin):
    w = y.shape[1]
    reps = w // LANES
    if reps > 1:
        cos = jnp.concatenate([cos] * reps, axis=1)
        sin = jnp.concatenate([sin] * reps, axis=1)
    lane = lax.broadcasted_iota(I32, y.shape, 1) & (HEAD_DIM - 1)
    half = ROPE_DIM // 2
    partner = jnp.where(lane < half, pltpu.roll(y, w - half, 1), pltpu.roll(y, half, 1))
    return y * cos + partner * sin


def _proj_kernel(x_ref, shift_ref, scale_ref, ng_ref, w_ref, cos_ref, sin_ref,
                 qa_ref, qat_ref, nsa_ref, win_ref, kslc_ref, vslct_ref, kwin_ref, vwint_ref, gate_ref, gatet_ref,
                 qb_ref, sb_ref, ksb_ref, vsb_ref, mg_ref):
    x = x_ref[...]
    h = _rms(x) * ng_ref[...]
    h = h * (1.0 + scale_ref[...]) + shift_ref[...]
    hb = h.astype(BF16)
    cos = cos_ref[...]
    sin = sin_ref[...]

    def mm(c0, c1):
        return _dot(hb, w_ref[:, c0:c1])

    def heads(dst, y, n):
        for i in range(n):
            dst[i] = y[:, i * HEAD_DIM:(i + 1) * HEAD_DIM].astype(dst.dtype)

    def heads_t(dst, y, n):
        yt = y.T
        for i in range(n):
            dst[i] = yt[i * HEAD_DIM:(i + 1) * HEAD_DIM, :].astype(dst.dtype)

    qa = _rope(mm(C_QA, C_QA + Q_NSA), cos, sin) * ATTN_SCALE
    qa_ref[...] = qa.astype(BF16)
    heads_t(qat_ref, qa, N_HEADS_NSA)

    kc = _rope(mm(C_KVA, C_KVA + 128), cos, sin)
    vc = mm(C_KVA + 128, C_KVA + 256)
    ks = _rope(mm(C_KVA + 256, C_KVA + 384), cos, sin)
    vs = mm(C_KVA + 384, C_KVA + 512)
    kw = _rope(mm(C_KVA + 512, C_KVA + 640), cos, sin)
    vw = mm(C_KVA + 640, C_KVA + 768)
    nsa_ref[:, 0:128] = kc
    nsa_ref[:, 128:256] = vc
    nsa_ref[:, 256:384] = ks
    nsa_ref[:, 384:512] = vs
    win_ref[:, 0:128] = kw
    win_ref[:, 128:256] = vw
    heads(kslc_ref, ks, N_KV_NSA)
    heads_t(vslct_ref, vs, N_KV_NSA)
    heads(kwin_ref, kw, N_KV_NSA)
    heads_t(vwint_ref, vw, N_KV_NSA)

    gates = jax.nn.sigmoid(mm(C_GA, C_GA + 2 * LANES))
    gate_ref[...] = gates
    gatet_ref[...] = gates.T

    qb_ref[...] = (mm(C_QB, C_QB + Q_SB) * ATTN_SCALE).astype(BF16)

    kb = mm(C_KVB, C_KVB + Q_SB)
    vb = mm(C_KVB + Q_SB, C_KVB + 2 * Q_SB)
    sb_ref[:, 0:Q_SB] = kb
    sb_ref[:, Q_SB:2 * Q_SB] = vb
    ksb_ref[...] = kb.astype(BF16)
    vsb_ref[...] = vb.astype(BF16)

    for j in range(2):
        mg_ref[:, j * D_MODEL:(j + 1) * D_MODEL] = jax.nn.sigmoid(
            mm(C_MG + j * D_MODEL, C_MG + (j + 1) * D_MODEL)).astype(BF16)


def _proj_call(x, shift, scale, ng, w, cos, sin, tm):
    rows = x.shape[0]
    per_row = shift.shape[0] != 1
    mod_spec = (pl.BlockSpec((tm, D_MODEL), lambda i: (i, 0)) if per_row
                else pl.BlockSpec((1, D_MODEL), lambda i: (0, 0)))
    row = lambda n: pl.BlockSpec((tm, n), lambda i: (i, 0))
    hm = lambda n: pl.BlockSpec((n, tm, HEAD_DIM), lambda i: (0, i, 0))
    hmt = lambda n: pl.BlockSpec((n, HEAD_DIM, tm), lambda i: (0, 0, i))
    sds = jax.ShapeDtypeStruct
    out_shape = (
        sds((rows, Q_NSA), BF16),
        sds((N_HEADS_NSA, HEAD_DIM, rows), BF16),
        sds((rows, 512), F32),
        sds((rows, 256), F32),
        sds((N_KV_NSA, rows, HEAD_DIM), BF16),
        sds((N_KV_NSA, HEAD_DIM, rows), BF16),
        sds((N_KV_NSA, rows, HEAD_DIM), BF16),
        sds((N_KV_NSA, HEAD_DIM, rows), BF16),
        sds((rows, 2 * LANES), F32),
        sds((2 * LANES, rows), F32),
        sds((rows, Q_SB), BF16),
        sds((rows, 2 * Q_SB), F32),
        sds((rows, Q_SB), BF16),
        sds((rows, Q_SB), BF16),
        sds((rows, 2 * D_MODEL), BF16),
    )
    out_specs = (row(Q_NSA), hmt(8), row(512), row(256), hm(2), hmt(2), hm(2), hmt(2), row(2 * LANES),
                 pl.BlockSpec((2 * LANES, tm), lambda i: (0, i)),
                 row(Q_SB), row(2 * Q_SB), row(Q_SB), row(Q_SB), row(2 * D_MODEL))
    return pl.pallas_call(
        _proj_kernel,
        grid=(rows // tm,),
        in_specs=[row(D_MODEL), mod_spec, mod_spec,
                  pl.BlockSpec((1, D_MODEL), lambda i: (0, 0)),
                  pl.BlockSpec((D_MODEL, C_END), lambda i: (0, 0)),
                  row(LANES), row(LANES)],
        out_specs=out_specs,
        out_shape=out_shape,
        compiler_params=_cparams(("arbitrary",)),
        name="proj_in",
    )(x, shift, scale, ng, w, cos, sin)


def _compress_kernel(pt_ref, pages_hbm, w1_ref, bias_ref, w2_ref, b2_ref, kc_ref, vct_ref,
                     buf, sem, hlo, hhi, *, n_pages, cp, paged_cache):
    b = pl.program_id(0)
    nch = n_pages // cp
    nh = cp * (PAGE // CMP_STRIDE)

    def copies(page, slot, k):
        dst = lambda c: buf.at[slot, c, pl.ds(k * PAGE, PAGE)]
        if paged_cache:
            return [pltpu.make_async_copy(pages_hbm.at[0, page, :, c, g, :],
                                          dst(c).at[:, pl.ds(g * HEAD_DIM, HEAD_DIM)], sem.at[slot])
                    for c in range(2) for g in range(N_KV_NSA)]
        return [pltpu.make_async_copy(pages_hbm.at[page, :, pl.ds(c * LANES, LANES)], dst(c), sem.at[slot])
                for c in range(2)]

    def start(ch, slot):
        for k in range(cp):
            for cpy in copies(pt_ref[b * n_pages + ch * cp + k], slot, k):
                cpy.start()

    def wait(slot):
        for k in range(cp):
            for cpy in copies(0, slot, k):
                cpy.wait()

    start(0, 0)
    for ch in range(nch):
        slot = ch % 2
        if ch + 1 < nch:
            start(ch + 1, 1 - slot)
        wait(slot)
        for cg in range(4):
            c, g = cg // N_KV_NSA, cg % N_KV_NSA
            acc = jnp.zeros((nh, 2 * CMP_HIDDEN), F32)
            for p in range(CMP_STRIDE):
                xp = buf[slot, c, pl.ds(p, nh, stride=CMP_STRIDE), :]
                acc = acc + _dot(xp[:, g * HEAD_DIM:(g + 1) * HEAD_DIM].astype(BF16), w1_ref[c, p])
            hlo[pl.ds(ch * nh, nh), cg * CMP_HIDDEN:(cg + 1) * CMP_HIDDEN] = acc[:, :CMP_HIDDEN]
            hhi[pl.ds(ch * nh, nh), cg * CMP_HIDDEN:(cg + 1) * CMP_HIDDEN] = acc[:, CMP_HIDDEN:]
    n_half = nch * nh
    hhi[pl.ds(n_half, SUBLANES), :] = jnp.zeros((SUBLANES, 4 * CMP_HIDDEN), F32)
    pre = hlo[...] + hhi[pl.ds(1, n_half), :] + bias_ref[...]
    act = jax.nn.gelu(pre).astype(BF16)
    out = _dot(act, w2_ref[...]) + b2_ref[...]
    out_t = out.T
    for g in range(N_KV_NSA):
        kc_ref[g] = out[:, g * HEAD_DIM:(g + 1) * HEAD_DIM].astype(BF16)
        vct_ref[g] = out_t[(2 + g) * HEAD_DIM:(3 + g) * HEAD_DIM, :].astype(BF16)


def _compress_call(page_table, pages, w1, bias, w2, b2):
    nb, n_pages = page_table.shape
    cp = min(32, n_pages)
    assert n_pages % cp == 0
    n_half = n_pages * (PAGE // CMP_STRIDE)
    kern = functools.partial(_compress_kernel, n_pages=n_pages, cp=cp, paged_cache=pages.ndim == 6)
    const = lambda shape: pl.BlockSpec(shape, lambda b, pt: (0,) * len(shape))
    grid_spec = pltpu.PrefetchScalarGridSpec(
        num_scalar_prefetch=1,
        grid=(nb,),
        in_specs=[pl.BlockSpec(memory_space=pl.ANY),
                  const((2, CMP_STRIDE, HEAD_DIM, 2 * CMP_HIDDEN)),
                  const((1, 4 * CMP_HIDDEN)),
                  const((4 * CMP_HIDDEN, 4 * HEAD_DIM)),
                  const((1, 4 * HEAD_DIM))],
        out_specs=(pl.BlockSpec((None, N_KV_NSA, n_half, HEAD_DIM), lambda b, pt: (b, 0, 0, 0)),
                   pl.BlockSpec((None, N_KV_NSA, HEAD_DIM, n_half), lambda b, pt: (b, 0, 0, 0))),
        scratch_shapes=[pltpu.VMEM((2, 2, cp * PAGE, LANES), F32),
                        pltpu.SemaphoreType.DMA((2,)),
                        pltpu.VMEM((n_half, 4 * CMP_HIDDEN), F32),
                        pltpu.VMEM((n_half + SUBLANES, 4 * CMP_HIDDEN), F32)],
    )
    return pl.pallas_call(
        kern, grid_spec=grid_spec,
        out_shape=(jax.ShapeDtypeStruct((nb, N_KV_NSA, n_half, HEAD_DIM), BF16),
                   jax.ShapeDtypeStruct((nb, N_KV_NSA, HEAD_DIM, n_half), BF16)),
        compiler_params=_cparams(("arbitrary",)),
        name="compress_kv",
    )(page_table.reshape(-1), pages, w1, bias, w2, b2)


def _compress_weights(cmp_pe, cmp_w1, cmp_b1, cmp_w2, cmp_b2):
    w1r = cmp_w1.reshape(2, CMP_LEN, HEAD_DIM, CMP_HIDDEN)
    w1 = jnp.concatenate([w1r[:, :CMP_STRIDE], w1r[:, CMP_STRIDE:]], axis=-1).astype(BF16)
    bias = jnp.einsum('cpd,cpdh->ch', cmp_pe, w1r, precision=lax.Precision.HIGHEST) + cmp_b1
    bias = jnp.repeat(bias, N_KV_NSA, axis=0).reshape(1, 4 * CMP_HIDDEN)
    w2 = jnp.zeros((4, CMP_HIDDEN, 4, HEAD_DIM), F32)
    for cg in range(4):
        w2 = w2.at[cg, :, cg, :].set(cmp_w2[cg // 2])
    w2 = w2.reshape(4 * CMP_HIDDEN, 4 * HEAD_DIM).astype(BF16)
    b2 = jnp.repeat(cmp_b2, N_KV_NSA, axis=0).reshape(1, 4 * HEAD_DIM)
    return w1, bias, w2, b2


def _masked_softmax(s, mask, axis=-1):
    s = jnp.where(mask, s, NEG_INF)
    m = jnp.max(s, axis=axis, keepdims=True)
    e = jnp.where(mask, jnp.exp(s - m), 0.0)
    return e / jnp.maximum(jnp.sum(e, axis=axis, keepdims=True), TINY)


def _tap_matrix(n_cmp_pad, n_sel_pad, axis):
    shape = (n_cmp_pad, n_sel_pad) if axis == 1 else (n_sel_pad, n_cmp_pad)
    n = lax.broadcasted_iota(I32, shape, 1 - axis)
    j = lax.broadcasted_iota(I32, shape, axis)
    front = CMP_LEN // CMP_STRIDE - 1
    return jnp.where((n >= SEL_RATIO * j - front) & (n <= SEL_RATIO * j + SEL_RATIO - 1), 1.0, 0.0).astype(BF16)


def _importance(p_sum, n_sel_pad, axis):
    taps = _tap_matrix(p_sum.shape[axis], n_sel_pad, axis)
    p1, p2, p3 = _split3(p_sum)
    if axis == 1:
        return _dot(p1, taps) + _dot(p2, taps) + _dot(p3, taps)
    return _dot(taps, p1) + _dot(taps, p2) + _dot(taps, p3)


def _topk_rounds(score, k, axis):
    n = score.shape[axis]
    blk = lax.broadcasted_iota(I32, score.shape, axis).astype(F32)
    outs = []
    work = score
    for _ in range(k):
        mx = jnp.max(work, axis=axis, keepdims=True)
        idx = jnp.min(jnp.where(work == mx, blk, float(n)), axis=axis, keepdims=True)
        outs.append((mx, idx))
        work = jnp.where(blk == idx, -jnp.inf, work)
    return outs


def _block_scores(imp, q_pos, n_sel, axis):
    blk = lax.broadcasted_iota(I32, imp.shape, axis)
    cur = q_pos // SEL_BLOCK
    forced = (blk == 0) | (blk == cur) | (blk == cur - 1)
    score = jnp.where(forced, BIG, jnp.where(blk <= cur, imp, NEG_INF))
    return jnp.where(blk < n_sel, score, -jnp.inf)


def _nsa_prompt_kernel(qt_ref, kc_ref, vct_ref, ks_ref, vst_ref, kw_ref, vwt_ref, gatet_ref, o_ref, bias_scr,
                       *, qb, kb, n_cmp, n_sel, topk):
    i = pl.program_id(1)
    n_cmp_pad = kc_ref.shape[0]
    nq = GROUP_NSA * qb
    qt = jnp.concatenate([qt_ref[r] for r in range(GROUP_NSA)], axis=1)
    q_pos1 = i * qb + lax.broadcasted_iota(I32, (1, qb), 1)
    q_pos = jnp.concatenate([q_pos1] * GROUP_NSA, axis=1)

    s = _dot(kc_ref[...], qt)
    n = lax.broadcasted_iota(I32, (n_cmp_pad, nq), 0)
    p_c = _masked_softmax(s, (n * CMP_STRIDE + CMP_LEN - 1 <= q_pos) & (n < n_cmp), axis=0)
    o_c = _dot(vct_ref[...], p_c.astype(BF16))

    p_sum = p_c[:, 0:qb]
    for r in range(1, GROUP_NSA):
        p_sum = p_sum + p_c[:, r * qb:(r + 1) * qb]
    imp = _importance(p_sum, n_sel, axis=0)
    score = _block_scores(imp, q_pos1, n_sel, axis=0)
    blk = lax.broadcasted_iota(I32, (n_sel, qb), 0).astype(F32)
    bias = jnp.full((n_sel, qb), NEG_INF, F32)
    for val, idx in _topk_rounds(score, topk, axis=0):
        bias = jnp.where((blk == idx) & (val > 0.5 * NEG_INF), 0.0, bias)
    bias_scr[...] = bias

    bpk = kb // SEL_BLOCK
    nkb = (i * qb + qb + kb - 1) // kb

    def sel_step(j, carry, causal):
        m, l, acc = carry
        start = pl.multiple_of(j * kb, kb)
        s = _dot(ks_ref[pl.ds(start, kb), :], qt)
        b8 = bias_scr[pl.ds(pl.multiple_of(j * bpk, bpk), bpk), :]
        b1 = jnp.broadcast_to(b8[:, None, :], (bpk, SEL_BLOCK, qb)).reshape(kb, qb)
        s = s + jnp.concatenate([b1] * GROUP_NSA, axis=1)
        if causal:
            k_pos = start + lax.broadcasted_iota(I32, (kb, nq), 0)
            s = jnp.where(k_pos <= q_pos, s, NEG_INF)
        m_new = jnp.maximum(m, jnp.max(s, axis=0, keepdims=True))
        e = jnp.exp(s - m_new)
        alpha = jnp.exp(m - m_new)
        l = alpha * l + jnp.sum(e, axis=0, keepdims=True)
        acc = alpha * acc + _dot(vst_ref[:, pl.ds(start, kb)], e.astype(BF16))
        return m_new, l, acc

    carry = (jnp.full((1, nq), NEG_INF, F32), jnp.zeros((1, nq), F32), jnp.zeros((HEAD_DIM, nq), F32))
    npair = (nkb - 1) // 2
    carry = lax.fori_loop(0, npair, lambda j, c: sel_step(2 * j + 1, sel_step(2 * j, c, False), False), carry)
    carry = lax.cond(nkb - 1 > 2 * npair, lambda c: sel_step(nkb - 2, c, False), lambda c: c, carry)
    _, l_s, acc_s = sel_step(nkb - 1, carry, True)
    o_s = acc_s / jnp.maximum(l_s, TINY)

    wlen = WINDOW + qb
    wstart = pl.multiple_of(jnp.maximum(i * qb - WINDOW, 0), qb)
    s = _dot(kw_ref[pl.ds(wstart, wlen), :], qt)
    dist = q_pos - (wstart + lax.broadcasted_iota(I32, (wlen, nq), 0))
    p_w = _masked_softmax(s, (dist >= 0) & (dist < WINDOW), axis=0)
    o_w = _dot(vwt_ref[:, pl.ds(wstart, wlen)], p_w.astype(BF16))

    gt = gatet_ref[...]
    gate = lambda br: jnp.concatenate([gt[3 * r + br:3 * r + br + 1, :] for r in range(GROUP_NSA)], axis=1)
    o = gate(0) * o_c + gate(1) * o_s + gate(2) * o_w
    o = jnp.concatenate([o[:, r * qb:(r + 1) * qb] for r in range(GROUP_NSA)], axis=0)
    o_ref[...] = o.T.astype(o_ref.dtype)


def _nsa_prompt_call(qat, kcmp, vcmpt, kslc, vslct, kwin, vwint, gatest, n_cmp):
    t = qat.shape[2]
    qb, kb = 128, 512
    assert t % kb == 0 and t >= WINDOW + qb
    n_sel = t // SEL_BLOCK
    n_cmp_pad = kcmp.shape[1]
    kern = functools.partial(_nsa_prompt_kernel, qb=qb, kb=kb, n_cmp=n_cmp, n_sel=n_sel, topk=min(TOP_N, n_sel))
    rows = lambda n: pl.BlockSpec((None, n, HEAD_DIM), lambda g, i: (g, 0, 0))
    cols = lambda n: pl.BlockSpec((None, HEAD_DIM, n), lambda g, i: (g, 0, 0))
    return pl.pallas_call(
        kern,
        grid=(N_KV_NSA, t // qb),
        in_specs=[pl.BlockSpec((GROUP_NSA, HEAD_DIM, qb), lambda g, i: (g, 0, i)),
                  rows(n_cmp_pad), cols(n_cmp_pad), rows(t), cols(t), rows(t), cols(t),
                  pl.BlockSpec((LANES, qb), lambda g, i: (g, i))],
        out_specs=pl.BlockSpec((qb, QG), lambda g, i: (i, g)),
        out_shape=jax.ShapeDtypeStruct((t, Q_NSA), BF16),
        scratch_shapes=[pltpu.VMEM((n_sel, qb), F32)],
        compiler_params=_cparams(("arbitrary", "arbitrary")),
        name="nsa_prompt",
    )(qat, kcmp, vcmpt, kslc, vslct, kwin, vwint, gatest)


def _sb_weights(z, mask, carry):
    kb = z.shape[1]
    l1 = -_softplus(z)
    if mask is not None:
        l1 = jnp.where(mask, l1, 0.0)
    sp = lax.broadcasted_iota(I32, (kb, kb), 0)
    sc = lax.broadcasted_iota(I32, (kb, kb), 1)
    later = jnp.where(sp > sc, 1.0, 0.0).astype(BF16)
    hi = l1.astype(BF16)
    lo = (l1 - hi.astype(F32)).astype(BF16)
    after = _dot(hi, later) + _dot(lo, later) + carry
    a = jnp.exp(z + l1 + after)
    if mask is not None:
        a = jnp.where(mask, a, 0.0)
    return a, carry + jnp.sum(l1, axis=-1, keepdims=True)


def _sb_prompt_kernel(q_ref, k_ref, v_ref, o_ref, *, qb, kb, nh):
    i = pl.program_id(1)
    width = nh * HEAD_DIM
    q = q_ref[...]
    own = [lax.broadcasted_iota(I32, (qb, width), 1) // HEAD_DIM == h for h in range(nh)]
    qh = [jnp.where(own[h], q, jnp.zeros_like(q)) for h in range(nh)]
    q_pos = i * qb + lax.broadcasted_iota(I32, (qb, 1), 0)

    def cond(st):
        return (st[0] >= 0) & (st[3] > SB_EXIT)

    def body(st):
        j, carries, acc, _ = st
        start = pl.multiple_of(j * kb, kb)
        k = k_ref[pl.ds(start, kb), :]
        v = v_ref[pl.ds(start, kb), :]
        mask = (start + lax.broadcasted_iota(I32, (qb, kb), 1)) < q_pos
        new_carries = []
        cmax = None
        for h in range(nh):
            a, c = _sb_weights(_dot_nt(qh[h], k), mask, carries[h])
            acc = acc + jnp.where(own[h], _dot(a.astype(BF16), v), 0.0)
            new_carries.append(c)
            cmax = c if cmax is None else jnp.maximum(cmax, c)
        return j - 1, tuple(new_carries), acc, jnp.max(cmax)

    j0 = (i * qb + qb - 1) // kb
    st = (j0, tuple(jnp.zeros((qb, 1), F32) for _ in range(nh)), jnp.zeros((qb, width), F32), jnp.float32(0.0))
    acc = lax.while_loop(cond, body, st)[2]
    o_ref[...] = acc.astype(o_ref.dtype)


def _sb_prompt_call(qb_all, ksb, vsb):
    t = qb_all.shape[0]
    qb, kb, nh = 128, 256, 4
    assert t % kb == 0
    width = nh * HEAD_DIM
    kern = functools.partial(_sb_prompt_kernel, qb=qb, kb=kb, nh=nh)
    whole = pl.BlockSpec((t, width), lambda h, i: (0, h))
    blk = pl.BlockSpec((qb, width), lambda h, i: (i, h))
    return pl.pallas_call(
        kern,
        grid=(N_HEADS_SB // nh, t // qb),
        in_specs=[blk, whole, whole],
        out_specs=blk,
        out_shape=jax.ShapeDtypeStruct((t, Q_SB), BF16),
        compiler_params=_cparams(("arbitrary", "arbitrary")),
        name="sb_prompt",
    )(qb_all, ksb, vsb)


def _merge_kernel(oa_ref, ob_ref, mg_ref, wa_ref, wb_ref, wo_ref, x_ref, gate_ref, ng1_ref, ng2_ref,
                  shift_ref, scale_ref, x1_ref, h2_ref):
    ya = _dot(oa_ref[...], wa_ref[...])
    yb = _dot(ob_ref[...], wb_ref[...])
    y = mg_ref[:, 0:D_MODEL].astype(F32) * ya + mg_ref[:, D_MODEL:2 * D_MODEL].astype(F32) * yb
    o = _dot(y.astype(BF16), wo_ref[...])
    x1 = x_ref[...] + gate_ref[...] * (_rms(o) * ng1_ref[...])
    x1_ref[...] = x1
    h2 = (_rms(x1) * ng2_ref[...]) * (1.0 + scale_ref[...]) + shift_ref[...]
    h2_ref[...] = h2.astype(BF16)


def _merge_call(oa, ob, mg, wa, wb, wo, x, gate, ng1, ng2, shift, scale, tm):
    rows = x.shape[0]
    per_row = gate.shape[0] != 1
    mod_spec = (pl.BlockSpec((tm, D_MODEL), lambda i: (i, 0)) if per_row
                else pl.BlockSpec((1, D_MODEL), lambda i: (0, 0)))
    vec = pl.BlockSpec((1, D_MODEL), lambda i: (0, 0))
    row = lambda n: pl.BlockSpec((tm, n), lambda i: (i, 0))
    wspec = lambda n: pl.BlockSpec((n, D_MODEL), lambda i: (0, 0))
    return pl.pallas_call(
        _merge_kernel,
        grid=(rows // tm,),
        in_specs=[row(Q_NSA), row(Q_SB), row(2 * D_MODEL), wspec(Q_NSA), wspec(Q_SB), wspec(D_MODEL),
                  row(D_MODEL), mod_spec, vec, vec, mod_spec, mod_spec],
        out_specs=(row(D_MODEL), row(D_MODEL)),
        out_shape=(jax.ShapeDtypeStruct((rows, D_MODEL), F32), jax.ShapeDtypeStruct((rows, D_MODEL), BF16)),
        compiler_params=_cparams(("arbitrary",)),
        name="merge_out",
    )(oa, ob, mg, wa, wb, wo, x, gate, ng1, ng2, shift, scale)


def _ffn_kernel(h_ref, wg_ref, wu_ref, wd_ref, x1_ref, gate_ref, ng_ref, o_ref, acc_ref):
    k = pl.program_id(1)

    @pl.when(k == 0)
    def _():
        acc_ref[...] = jnp.zeros_like(acc_ref)

    h = h_ref[...]
    g = _dot(h, wg_ref[...])
    u = _dot(h, wu_ref[...])
    act = (g * jax.nn.sigmoid(g)) * u
    acc_ref[...] += _dot(act.astype(BF16), wd_ref[...])

    @pl.when(k == pl.num_programs(1) - 1)
    def _():
        o_ref[...] = x1_ref[...] + gate_ref[...] * (_rms(acc_ref[...]) * ng_ref[...])


def _ffn_call(h2, w_up, w_down, x1, gate, ng, tm):
    rows = h2.shape[0]
    d_ff = w_down.shape[0]
    tk = 256
    nk = d_ff // tk
    assert d_ff % tk == 0
    per_row = gate.shape[0] != 1
    mod_spec = (pl.BlockSpec((tm, D_MODEL), lambda i, k: (i, 0)) if per_row
                else pl.BlockSpec((1, D_MODEL), lambda i, k: (0, 0)))
    row = pl.BlockSpec((tm, D_MODEL), lambda i, k: (i, 0))
    return pl.pallas_call(
        _ffn_kernel,
        grid=(rows // tm, nk),
        in_specs=[row,
                  pl.BlockSpec((D_MODEL, tk), lambda i, k: (0, k)),
                  pl.BlockSpec((D_MODEL, tk), lambda i, k: (0, nk + k)),
                  pl.BlockSpec((tk, D_MODEL), lambda i, k: (k, 0)),
                  row, mod_spec,
                  pl.BlockSpec((1, D_MODEL), lambda i, k: (0, 0))],
        out_specs=row,
        out_shape=jax.ShapeDtypeStruct((rows, D_MODEL), F32),
        scratch_shapes=[pltpu.VMEM((tm, D_MODEL), F32)],
        compiler_params=_cparams(("arbitrary", "arbitrary")),
        name="ffn",
    )(h2, w_up, w_up, w_down, x1, gate, ng)


def _nsa_s1_kernel(q_ref, kc_ref, vct_ref, oc_ref, idx_ref, *, nt, past_len, n_cmp, n_sel, topk):
    n_cmp_pad = kc_ref.shape[1]
    n_sel_pad = ((n_sel + LANES - 1) // LANES) * LANES
    rows = GROUP_NSA * nt
    q_pos = past_len + lax.broadcasted_iota(I32, (rows, 1), 0) % nt
    n = lax.broadcasted_iota(I32, (rows, n_cmp_pad), 1)
    mask = (n * CMP_STRIDE + CMP_LEN - 1 <= q_pos) & (n < n_cmp)
    probs = []
    for g in range(N_KV_NSA):
        q = q_ref[g * rows:(g + 1) * rows, :]
        p = _masked_softmax(_dot_nt(q, kc_ref[g]), mask)
        oc_ref[g * rows:(g + 1) * rows, :] = _dot_nt(p.astype(BF16), vct_ref[g])
        probs.append(p)
    p_all = jnp.concatenate(probs, axis=0)
    ro = lax.broadcasted_iota(I32, (N_KV_NSA * nt, N_KV_NSA * rows), 0)
    ri = lax.broadcasted_iota(I32, (N_KV_NSA * nt, N_KV_NSA * rows), 1)
    pick = jnp.where((ro // nt == ri // rows) & (ro % nt == ri % nt), 1.0, 0.0).astype(BF16)
    p1, p2, p3 = _split3(p_all)
    p_sum = _dot(pick, p1) + _dot(pick, p2) + _dot(pick, p3)
    imp = _importance(p_sum, n_sel_pad, axis=1)
    t_pos = past_len + lax.broadcasted_iota(I32, (N_KV_NSA * nt, 1), 0) % nt
    score = _block_scores(imp, t_pos, n_sel, axis=1)
    lane = lax.broadcasted_iota(I32, (N_KV_NSA * nt, LANES), 1)
    out = jnp.full((N_KV_NSA * nt, LANES), -1.0, F32)
    for r, (val, idx) in enumerate(_topk_rounds(score, topk, axis=1)):
        out = jnp.where(lane == r, jnp.where(val > 0.5 * NEG_INF, idx, -1.0), out)
    idx_ref[...] = out.astype(I32)


def _nsa_s1_call(q1, kcmp, vcmpt, nt, past_len, n_cmp, n_sel):
    nb = q1.shape[0]
    n_half = kcmp.shape[2]
    kern = functools.partial(_nsa_s1_kernel, nt=nt, past_len=past_len, n_cmp=n_cmp, n_sel=n_sel,
                             topk=min(TOP_N, n_sel))
    rows = N_HEADS_NSA * nt
    per_b = lambda *shape: pl.BlockSpec((None,) + shape, lambda b: (b,) + (0,) * len(shape))
    return pl.pallas_call(
        kern,
        grid=(nb,),
        in_specs=[per_b(rows, HEAD_DIM), per_b(N_KV_NSA, n_half, HEAD_DIM), per_b(N_KV_NSA, HEAD_DIM, n_half)],
        out_specs=(per_b(rows, HEAD_DIM), per_b(N_KV_NSA * nt, LANES)),
        out_shape=(jax.ShapeDtypeStruct((nb, rows, HEAD_DIM), F32),
                   jax.ShapeDtypeStruct((nb, N_KV_NSA * nt, LANES), I32)),
        compiler_params=_cparams(("arbitrary",)),
        name="nsa_sample_topk",
    )(q1, kcmp, vcmpt)


def _nsa_s2_kernel(idx_s, pt_s, q_ref, idxv_ref, oc_ref, gate_ref, newkv_ref, wkv_ref, wnew_ref, cache_hbm,
                   o_ref, kbuf, vbuf, sem, os_scr, *, nt, n_pages, n_past_blocks, topk):
    b = pl.program_id(0)
    ngt = N_KV_NSA * nt
    bpp = PAGE // SEL_BLOCK

    def slot_copies(j, gt, s):
        page = pt_s[b * n_pages + j // bpp]
        src = lambda kind: cache_hbm.at[0, page, pl.ds((j % bpp) * SEL_BLOCK, SEL_BLOCK), kind, gt // nt, :]
        dst = lambda buf: buf.at[gt, pl.ds(s * SEL_BLOCK, SEL_BLOCK)]
        return (pltpu.make_async_copy(src(2), dst(kbuf), sem.at[0]),
                pltpu.make_async_copy(src(3), dst(vbuf), sem.at[0]))

    def issue(n, _):
        gt, s = n // topk, n % topk
        j = idx_s[(b * ngt + gt) * topk + s]
        valid = (j >= 0) & (j < n_past_blocks)

        @pl.when(valid)
        def _():
            for cpy in slot_copies(j, gt, s):
                cpy.start()

        @pl.when(jnp.logical_not(valid))
        def _():
            zeros = jnp.zeros((SEL_BLOCK, HEAD_DIM), F32)
            kbuf[gt, pl.ds(s * SEL_BLOCK, SEL_BLOCK), :] = zeros
            vbuf[gt, pl.ds(s * SEL_BLOCK, SEL_BLOCK), :] = zeros
        return 0

    def drain(n, _):
        gt, s = n // topk, n % topk
        j = idx_s[(b * ngt + gt) * topk + s]

        @pl.when((j >= 0) & (j < n_past_blocks))
        def _():
            for cpy in slot_copies(j, gt, s):
                cpy.wait()
        return 0

    lax.fori_loop(0, ngt * topk, issue, 0)

    rows = GROUP_NSA * nt
    t_row = lax.broadcasted_iota(I32, (rows, 1), 0) // GROUP_NSA
    nw = wkv_ref.shape[1]
    nn = newkv_ref.shape[1]
    new_ok = lax.broadcasted_iota(I32, (rows, nn), 1) <= t_row
    win_ok = lax.broadcasted_iota(I32, (rows, nw), 1) > t_row
    o_w = []
    for g in range(N_KV_NSA):
        q = q_ref[g * rows:(g + 1) * rows, :]
        s_old = jnp.where(win_ok, _dot_nt(q, wkv_ref[g]), NEG_INF)
        s_new = jnp.where(new_ok, _dot_nt(q, wnew_ref[g]), NEG_INF)
        m = jnp.maximum(jnp.max(s_old, axis=-1, keepdims=True), jnp.max(s_new, axis=-1, keepdims=True))
        e_old = jnp.where(win_ok, jnp.exp(s_old - m), 0.0)
        e_new = jnp.where(new_ok, jnp.exp(s_new - m), 0.0)
        l = jnp.sum(e_old, axis=-1, keepdims=True) + jnp.sum(e_new, axis=-1, keepdims=True)
        pv = _dot(e_old.astype(BF16), wkv_ref[g]) + _dot(e_new.astype(BF16), wnew_ref[g])
        o_w.append(pv[:, HEAD_DIM:] / jnp.maximum(l, TINY))
    o_w = jnp.concatenate(o_w, axis=0)

    lax.fori_loop(0, ngt * topk, drain, 0)

    idxv = idxv_ref[...]
    slot_valid = jnp.where((idxv >= 0) & (idxv < n_past_blocks), 1.0, 0.0).astype(BF16)
    es = lax.broadcasted_iota(I32, (LANES, topk * SEL_BLOCK), 0)
    ek = lax.broadcasted_iota(I32, (LANES, topk * SEL_BLOCK), 1)
    expand = jnp.where(es == ek // SEL_BLOCK, 1.0, 0.0).astype(BF16)
    key_valid = _dot(slot_valid, expand) > 0.5
    for gt in range(ngt):
        g, t = gt // nt, gt % nt
        q = q_ref[gt * GROUP_NSA:(gt + 1) * GROUP_NSA, :]
        kvn = newkv_ref[g]
        ok_old = key_valid[gt:gt + 1, :]
        ok_new = lax.broadcasted_iota(I32, (GROUP_NSA, nn), 1) <= t
        s_old = jnp.where(ok_old, _dot_nt(q[:, :HEAD_DIM], kbuf[gt].astype(BF16)), NEG_INF)
        s_new = jnp.where(ok_new, _dot_nt(q, kvn), NEG_INF)
        m = jnp.maximum(jnp.max(s_old, axis=-1, keepdims=True), jnp.max(s_new, axis=-1, keepdims=True))
        e_old = jnp.where(ok_old, jnp.exp(s_old - m), 0.0)
        e_new = jnp.where(ok_new, jnp.exp(s_new - m), 0.0)
        l = jnp.sum(e_old, axis=-1, keepdims=True) + jnp.sum(e_new, axis=-1, keepdims=True)
        pv = _dot(e_old.astype(BF16), vbuf[gt].astype(BF16)) + _dot(e_new.astype(BF16), kvn)[:, HEAD_DIM:]
        os_scr[gt * GROUP_NSA:(gt + 1) * GROUP_NSA, :] = pv / jnp.maximum(l, TINY)

    gates = gate_ref[...]
    o_ref[...] = gates[:, 0:1] * oc_ref[...] + gates[:, 1:2] * os_scr[...] + gates[:, 2:3] * o_w


def _nsa_s2_call(idx, page_table, q2, oc, gates, newkv, wkv, wnew, cache, nt, past_len):
    nb, n_pages = page_table.shape
    topk = TOP_N
    ngt = N_KV_NSA * nt
    rows = N_HEADS_NSA * nt
    kern = functools.partial(_nsa_s2_kernel, nt=nt, n_pages=n_pages, n_past_blocks=past_len // SEL_BLOCK, topk=topk)
    per_b = lambda *shape: pl.BlockSpec((None,) + shape, lambda b, i_s, p_s: (b,) + (0,) * len(shape))
    grid_spec = pltpu.PrefetchScalarGridSpec(
        num_scalar_prefetch=2,
        grid=(nb,),
        in_specs=[per_b(rows, LANES), per_b(ngt, LANES), per_b(rows, HEAD_DIM), per_b(rows, LANES),
                  per_b(N_KV_NSA, newkv.shape[2], LANES), per_b(N_KV_NSA, wkv.shape[2], LANES),
                  per_b(N_KV_NSA, wnew.shape[2], LANES), pl.BlockSpec(memory_space=pl.ANY)],
        out_specs=per_b(rows, HEAD_DIM),
        scratch_shapes=[pltpu.VMEM((ngt, topk * SEL_BLOCK, HEAD_DIM), F32),
                        pltpu.VMEM((ngt, topk * SEL_BLOCK, HEAD_DIM), F32),
                        pltpu.SemaphoreType.DMA((1,)),
                        pltpu.VMEM((rows, HEAD_DIM), F32)],
    )
    return pl.pallas_call(
        kern, grid_spec=grid_spec,
        out_shape=jax.ShapeDtypeStruct((nb, rows, HEAD_DIM), F32),
        compiler_params=_cparams(("arbitrary",)),
        name="nsa_sample_gather",
    )(idx[:, :, :topk].reshape(-1), page_table.reshape(-1), q2, idx, oc, gates, newkv, wkv, wnew, cache)


def _sb_sample_kernel(pt_s, q_ref, newk_ref, newv_ref, cache_hbm, o_ref, buf, sem, *, nt, n_pages):
    b = pl.program_id(0)
    nh = N_HEADS_SB
    rows = nh * nt
    q = q_ref[...]
    row_h = lax.broadcasted_iota(I32, (rows, PAGE), 0) // nt
    row_hw = lax.broadcasted_iota(I32, (rows, nh * PAGE), 0) // nt
    lane_hw = lax.broadcasted_iota(I32, (rows, nh * PAGE), 1) // PAGE

    def step(k_all, v_all, mask, carry, acc):
        z_full = _dot_nt(q, k_all)
        z = jnp.zeros((rows, PAGE), F32)
        for h in range(nh):
            z = jnp.where(row_h == h, z_full[:, h * PAGE:(h + 1) * PAGE], z)
        a, carry = _sb_weights(z, mask, carry)
        a_wide = jnp.where(row_hw == lane_hw, jnp.concatenate([a] * nh, axis=1), 0.0)
        return carry, acc + _dot(a_wide.astype(BF16), v_all)

    def copies(pg, slot):
        page = pt_s[b * n_pages + pg]
        return [pltpu.make_async_copy(cache_hbm.at[0, page, :, kv, h, :], buf.at[slot, kv * nh + h], sem.at[slot])
                for kv in range(2) for h in range(nh)]

    for cpy in copies(n_pages - 1, 0):
        cpy.start()
    t_row = lax.broadcasted_iota(I32, (rows, PAGE), 0) % nt
    new_mask = lax.broadcasted_iota(I32, (rows, PAGE), 1) < t_row
    carry, acc = step(newk_ref[...].reshape(nh * PAGE, HEAD_DIM), newv_ref[...].reshape(nh * PAGE, HEAD_DIM),
                      new_mask, jnp.zeros((rows, 1), F32), jnp.zeros((rows, HEAD_DIM), F32))

    def cond(st):
        return (st[0] >= 0) & (st[4] > SB_EXIT)

    def body(st):
        pg, slot, carry, acc, _ = st
        for cpy in copies(pg, slot):
            cpy.wait()

        @pl.when(pg > 0)
        def _():
            for cpy in copies(pg - 1, 1 - slot):
                cpy.start()

        k_all = buf[slot, 0:nh].reshape(nh * PAGE, HEAD_DIM).astype(BF16)
        v_all = buf[slot, nh:2 * nh].reshape(nh * PAGE, HEAD_DIM).astype(BF16)
        carry, acc = step(k_all, v_all, None, carry, acc)
        return pg - 1, 1 - slot, carry, acc, jnp.max(carry)

    pg, slot, _, acc, _ = lax.while_loop(cond, body, (jnp.int32(n_pages - 1), jnp.int32(0), carry, acc, jnp.max(carry)))

    @pl.when(pg >= 0)
    def _():
        for cpy in copies(pg, slot):
            cpy.wait()

    o_ref[...] = acc


def _sb_sample_call(page_table, q, newk, newv, cache, nt):
    nb, n_pages = page_table.shape
    nh = N_HEADS_SB
    kern = functools.partial(_sb_sample_kernel, nt=nt, n_pages=n_pages)
    per_b = lambda *shape: pl.BlockSpec((None,) + shape, lambda b, p_s: (b,) + (0,) * len(shape))
    grid_spec = pltpu.PrefetchScalarGridSpec(
        num_scalar_prefetch=1,
        grid=(nb,),
        in_specs=[per_b(nh * nt, HEAD_DIM), per_b(nh, PAGE, HEAD_DIM), per_b(nh, PAGE, HEAD_DIM),
                  pl.BlockSpec(memory_space=pl.ANY)],
        out_specs=per_b(nh * nt, HEAD_DIM),
        scratch_shapes=[pltpu.VMEM((2, 2 * nh, PAGE, HEAD_DIM), F32), pltpu.SemaphoreType.DMA((2,))],
    )
    return pl.pallas_call(
        kern, grid_spec=grid_spec,
        out_shape=jax.ShapeDtypeStruct((nb, nh * nt, HEAD_DIM), F32),
        compiler_params=_cparams(("arbitrary",)),
        name="sb_sample",
    )(page_table.reshape(-1), q, newk, newv, cache)


def _rope_tables(pos):
    half = ROPE_DIM // 2
    inv_freq = ROPE_THETA ** (-jnp.arange(half, dtype=F32) / half)
    ang = pos.astype(F32)[:, None] * inv_freq[None, :]
    cos, sin = jnp.cos(ang), jnp.sin(ang)
    one = jnp.ones((pos.shape[0], HEAD_DIM - ROPE_DIM), F32)
    c64 = jnp.concatenate([cos, cos, one], axis=1)
    s64 = jnp.concatenate([-sin, sin, jnp.zeros_like(one)], axis=1)
    return jnp.tile(c64, (1, LANES // HEAD_DIM)), jnp.tile(s64, (1, LANES // HEAD_DIM))


def _pad_w_in(w_in):
    c_ga = 512 + 768
    n_ga = 3 * N_HEADS_NSA
    ga = w_in[:, c_ga:c_ga + n_ga].reshape(D_MODEL, N_KV_NSA, n_ga // N_KV_NSA)
    ga = jnp.pad(ga, ((0, 0), (0, 0), (0, LANES - n_ga // N_KV_NSA))).reshape(D_MODEL, N_KV_NSA * LANES)
    return jnp.concatenate([w_in[:, :c_ga], ga, w_in[:, c_ga + n_ga:]], axis=1).astype(BF16)


def _pad_axis(x, axis, size):
    pad = [(0, 0)] * x.ndim
    pad[axis] = (0, size - x.shape[axis])
    return jnp.pad(x, pad)


def kernel(x_prompt, x_sample, cache_nsa_kv, cache_sb_kv, state_win_kv, page_table, c_prompt, c_sample, ada_w, ada_b, norm_g, w_in, cmp_pe, cmp_w1, cmp_b1, cmp_w2, cmp_b2, w_o_nsa, w_o_sb, w_out, ffn_w_up, ffn_w_down):
    assert ada_w.shape[0] == 1 and x_prompt.shape[0] == 1
    t = x_prompt.shape[1]
    nb, nt = x_sample.shape[:2]
    n_pages = page_table.shape[1]
    past_len = n_pages * PAGE
    assert nt <= 8 and state_win_kv.shape[2] == WINDOW and t >= WINDOW

    w = _pad_w_in(w_in[0])
    cw = _compress_weights(cmp_pe[0], cmp_w1[0], cmp_b1[0], cmp_w2[0], cmp_b2[0])
    wa = w_o_nsa[0].astype(BF16)
    wb = w_o_sb[0].astype(BF16)
    wo = w_out[0].astype(BF16)
    wup = ffn_w_up[0].astype(BF16)
    wdn = ffn_w_down[0].astype(BF16)
    ng = norm_g[0]

    c_all = _pad_axis(jnp.concatenate([c_prompt, c_sample], axis=0), 0, -(-(1 + nb) // SUBLANES) * SUBLANES)
    mod = _mod_call(c_all, ada_w[0], ada_b[0][None])
    modp = [mod[0:1, k * D_MODEL:(k + 1) * D_MODEL] for k in range(6)]
    mods = [jnp.repeat(mod[1:1 + nb, k * D_MODEL:(k + 1) * D_MODEL], nt, axis=0) for k in range(6)]

    xp = x_prompt[0]
    cos, sin = _rope_tables(jnp.arange(t))
    (_, qat, nsa_rows, win_rows, kslc, vslct, kwin, vwint, _, gatest, qb, sb_rows, ksb, vsb, mg) = _proj_call(
        xp, modp[0], modp[1], ng[0:1], w, cos, sin, tm=256)
    ident = jnp.arange(t // PAGE, dtype=I32).reshape(1, -1)
    kcmp, vcmpt = _compress_call(ident, nsa_rows.reshape(t // PAGE, PAGE, 512), *cw)
    oa = _nsa_prompt_call(qat, kcmp[0], vcmpt[0], kslc, vslct, kwin, vwint, gatest, n_cmp=t // CMP_STRIDE - 1)
    ob = _sb_prompt_call(qb, ksb, vsb)
    x1, h2 = _merge_call(oa, ob, mg, wa, wb, wo, xp, modp[2], ng[1:2], ng[2:3], modp[3], modp[4], tm=256)
    y_prompt = _ffn_call(h2, wup, wdn, x1, modp[5], ng[3:4], tm=512)

    rs = nb * nt
    xs = x_sample.reshape(rs, D_MODEL)
    cos_s, sin_s = _rope_tables(past_len + jnp.tile(jnp.arange(nt), nb))
    (qa_s, _, nsa_new, win_new, _, _, _, _, gates_s, _, qb_s, sb_new, _, _, mg_s) = _proj_call(
        xs, mods[0], mods[1], ng[0:1], w, cos_s, sin_s, tm=rs)

    cmp_pages = cache_nsa_kv[0, :, :, 0:2].reshape(cache_nsa_kv.shape[1], PAGE, 4 * HEAD_DIM)
    kcmp_s, vcmpt_s = _compress_call(page_table, cmp_pages, *cw)
    n_cmp = past_len // CMP_STRIDE - 1
    n_sel = past_len // SEL_BLOCK + 1

    qa5 = qa_s.reshape(nb, nt, N_KV_NSA, GROUP_NSA, HEAD_DIM)
    q1 = qa5.transpose(0, 2, 3, 1, 4).reshape(nb, N_HEADS_NSA * nt, HEAD_DIM)
    oc, idx = _nsa_s1_call(q1, kcmp_s, vcmpt_s, nt, past_len, n_cmp, n_sel)
    q2 = qa5.transpose(0, 2, 1, 3, 4).reshape(nb, N_HEADS_NSA * nt, HEAD_DIM)
    q2 = _pad_axis(q2, 2, LANES)
    oc2 = oc.reshape(nb, N_KV_NSA, GROUP_NSA, nt, HEAD_DIM).transpose(0, 1, 3, 2, 4).reshape(nb, -1, HEAD_DIM)
    g2 = gates_s.reshape(nb, nt, N_KV_NSA, LANES)[..., :3 * GROUP_NSA].reshape(nb, nt, N_KV_NSA, GROUP_NSA, 3)
    g2 = _pad_axis(g2.transpose(0, 2, 1, 3, 4).reshape(nb, N_HEADS_NSA * nt, 3), 2, LANES)

    def new_rows_kv(rows_kv):
        kv = rows_kv.reshape(nb, nt, 2, N_KV_NSA, HEAD_DIM).transpose(0, 3, 1, 2, 4).reshape(nb, N_KV_NSA, nt, LANES)
        return _pad_axis(kv, 2, SUBLANES).astype(BF16)

    newkv = new_rows_kv(nsa_new[:, 256:512])
    wnew = new_rows_kv(win_new)
    wkv = state_win_kv[0].transpose(0, 3, 1, 2, 4).reshape(nb, N_KV_NSA, WINDOW, LANES).astype(BF16)
    oa_s = _nsa_s2_call(idx, page_table, q2, oc2, g2, newkv, wkv, wnew, cache_nsa_kv, nt, past_len)
    oa_s = oa_s.reshape(nb, N_KV_NSA, nt, GROUP_NSA, HEAD_DIM).transpose(0, 2, 1, 3, 4).reshape(rs, Q_NSA).astype(BF16)

    q_sb = qb_s.reshape(nb, nt, N_HEADS_SB, HEAD_DIM).transpose(0, 2, 1, 3).reshape(nb, N_HEADS_SB * nt, HEAD_DIM)
    new_sb = sb_new.reshape(nb, nt, 2, N_HEADS_SB, HEAD_DIM).transpose(2, 0, 3, 1, 4)
    new_sb = _pad_axis(new_sb, 3, PAGE).astype(BF16)
    ob_s = _sb_sample_call(page_table, q_sb, new_sb[0], new_sb[1], cache_sb_kv, nt)
    ob_s = ob_s.reshape(nb, N_HEADS_SB, nt, HEAD_DIM).transpose(0, 2, 1, 3).reshape(rs, Q_SB).astype(BF16)

    x1s, h2s = _merge_call(oa_s, ob_s, mg_s, wa, wb, wo, xs, mods[2], ng[1:2], ng[2:3], mods[3], mods[4], tm=rs)
    y_sample = _ffn_call(h2s, wup, wdn, x1s, mods[5], ng[3:4], tm=rs)

    win_new5 = win_new.reshape(nb, nt, 2, N_KV_NSA, HEAD_DIM)
    win_kv_sample = jnp.concatenate([state_win_kv[0], win_new5], axis=1)[:, nt:][:, -WINDOW:]
    return (y_prompt[None],
            y_sample.reshape(nb, nt, D_MODEL),
            nsa_rows.reshape(1, 1, t, 4, N_KV_NSA, HEAD_DIM),
            sb_rows.reshape(1, 1, t, 2, N_HEADS_SB, HEAD_DIM),
            win_rows[t - WINDOW:].reshape(1, 1, WINDOW, 2, N_KV_NSA, HEAD_DIM),
            nsa_new.reshape(1, nb, nt, 4, N_KV_NSA, HEAD_DIM),
            sb_new.reshape(1, nb, nt, 2, N_HEADS_SB, HEAD_DIM),
            win_kv_sample[None])
```

```python
import functools

import jax
import jax.numpy as jnp
from jax import lax
from jax.experimental import pallas as pl
from jax.experimental.pallas import tpu as pltpu

F32 = jnp.float32
BF16 = jnp.bfloat16
I32 = jnp.int32

D_MODEL = 1024
HEAD_DIM = 64
N_HEADS_NSA = 8
N_KV_NSA = 2
GROUP_NSA = N_HEADS_NSA // N_KV_NSA
N_HEADS_SB = 8
ROPE_DIM = HEAD_DIM // 4
ROPE_THETA = 500000.0
CMP_STRIDE = 16
CMP_LEN = 2 * CMP_STRIDE
CMP_HIDDEN = 4 * HEAD_DIM
SEL_BLOCK = 64
SEL_RATIO = SEL_BLOCK // CMP_STRIDE
TOP_N = 16
WINDOW = 512
PAGE = 128
RMS_EPS = 1e-6
NEG_INF = -1e30
BIG = 1e30
TINY = 1e-30
ATTN_SCALE = HEAD_DIM ** -0.5

LANES = 128
SUBLANES = 8
VMEM_LIMIT = 56 * 1024 * 1024
SB_EXIT = -104.0

C_QA = 0
C_KVA = 512
C_GA = 1280
C_QB = 1536
C_KVB = 2048
C_MG = 3072
C_END = 5120

Q_NSA = N_HEADS_NSA * HEAD_DIM
Q_SB = N_HEADS_SB * HEAD_DIM
QG = GROUP_NSA * HEAD_DIM


def _cparams(sem):
    return pltpu.CompilerParams(dimension_semantics=sem, vmem_limit_bytes=VMEM_LIMIT)


def _rms(x):
    return x * lax.rsqrt(jnp.mean(x * x, axis=-1, keepdims=True) + RMS_EPS)


def _dot(a, b):
    return jnp.dot(a, b, preferred_element_type=F32)


def _dot_nt(a, b):
    return lax.dot_general(a, b, (((1,), (1,)), ((), ())), preferred_element_type=F32)


def _split3(x):
    p1 = x.astype(BF16)
    r1 = x - p1.astype(F32)
    p2 = r1.astype(BF16)
    p3 = (r1 - p2.astype(F32)).astype(BF16)
    return p1, p2, p3


def _softplus(z):
    return jnp.maximum(z, 0.0) + jnp.log(1.0 + jnp.exp(-jnp.abs(z)))


def _mod_kernel(c_ref, w_ref, b_ref, o_ref):
    c = c_ref[...]
    s = c * jax.nn.sigmoid(c)
    o_ref[...] = _dot(s.astype(BF16), w_ref[...].astype(BF16)) + b_ref[...]


def _mod_call(c, w, b):
    rows, n = c.shape[0], w.shape[1]
    tn = 1536
    return pl.pallas_call(
        _mod_kernel,
        grid=(n // tn,),
        in_specs=[pl.BlockSpec((rows, D_MODEL), lambda j: (0, 0)),
                  pl.BlockSpec((D_MODEL, tn), lambda j: (0, j)),
                  pl.BlockSpec((1, tn), lambda j: (0, j))],
        out_specs=pl.BlockSpec((rows, tn), lambda j: (0, j)),
        out_shape=jax.ShapeDtypeStruct((rows, n), F32),
        compiler_params=_cparams(("arbitrary",)),
        name="adaln_mod",
    )(c, w, b)


def _rope(y, cos, sin):
    w = y.shape[1]
    reps = w // LANES
    if reps > 1:
        cos = jnp.concatenate([cos] * reps, axis=1)
        sin = jnp.concatenate([sin] * reps, axis=1)
    lane = lax.broadcasted_iota(I32, y.shape, 1) & (HEAD_DIM - 1)
    half = ROPE_DIM // 2
    partner = jnp.where(lane < half, pltpu.roll(y, w - half, 1), pltpu.roll(y, half, 1))
    return y * cos + partner * sin


def _proj_kernel(x_ref, shift_ref, scale_ref, ng_ref, w_ref, cos_ref, sin_ref,
                 qa_ref, qat_ref, nsa_ref, win_ref, kslc_ref, vslct_ref, kwin_ref, vwint_ref, gate_ref, gatet_ref,
                 qb_ref, sb_ref, ksb_ref, vsb_ref, mg_ref):
    x = x_ref[...]
    h = _rms(x) * ng_ref[...]
    h = h * (1.0 + scale_ref[...]) + shift_ref[...]
    hb = h.astype(BF16)
    cos = cos_ref[...]
    sin = sin_ref[...]

    def mm(c0, c1):
        return _dot(hb, w_ref[:, c0:c1])

    def heads(dst, y, n):
        for i in range(n):
            dst[i] = y[:, i * HEAD_DIM:(i + 1) * HEAD_DIM].astype(dst.dtype)

    def heads_t(dst, y, n):
        yt = y.T
        for i in range(n):
            dst[i] = yt[i * HEAD_DIM:(i + 1) * HEAD_DIM, :].astype(dst.dtype)

    qa = _rope(mm(C_QA, C_QA + Q_NSA), cos, sin) * ATTN_SCALE
    qa_ref[...] = qa.astype(BF16)
    heads_t(qat_ref, qa, N_HEADS_NSA)

    kc = _rope(mm(C_KVA, C_KVA + 128), cos, sin)
    vc = mm(C_KVA + 128, C_KVA + 256)
    ks = _rope(mm(C_KVA + 256, C_KVA + 384), cos, sin)
    vs = mm(C_KVA + 384, C_KVA + 512)
    kw = _rope(mm(C_KVA + 512, C_KVA + 640), cos, sin)
    vw = mm(C_KVA + 640, C_KVA + 768)
    nsa_ref[:, 0:128] = kc
    nsa_ref[:, 128:256] = vc
    nsa_ref[:, 256:384] = ks
    nsa_ref[:, 384:512] = vs
    win_ref[:, 0:128] = kw
    win_ref[:, 128:256] = vw
    heads(kslc_ref, ks, N_KV_NSA)
    heads_t(vslct_ref, vs, N_KV_NSA)
    heads(kwin_ref, kw, N_KV_NSA)
    heads_t(vwint_ref, vw, N_KV_NSA)

    gates = jax.nn.sigmoid(mm(C_GA, C_GA + 2 * LANES))
    gate_ref[...] = gates
    gatet_ref[...] = gates.T

    qb_ref[...] = (mm(C_QB, C_QB + Q_SB) * ATTN_SCALE).astype(BF16)

    kb = mm(C_KVB, C_KVB + Q_SB)
    vb = mm(C_KVB + Q_SB, C_KVB + 2 * Q_SB)
    sb_ref[:, 0:Q_SB] = kb
    sb_ref[:, Q_SB:2 * Q_SB] = vb
    ksb_ref[...] = kb.astype(BF16)
    vsb_ref[...] = vb.astype(BF16)

    for j in range(2):
        mg_ref[:, j * D_MODEL:(j + 1) * D_MODEL] = jax.nn.sigmoid(
            mm(C_MG + j * D_MODEL, C_MG + (j + 1) * D_MODEL)).astype(BF16)


def _proj_call(x, shift, scale, ng, w, cos, sin, tm):
    rows = x.shape[0]
    per_row = shift.shape[0] != 1
    mod_spec = (pl.BlockSpec((tm, D_MODEL), lambda i: (i, 0)) if per_row
                else pl.BlockSpec((1, D_MODEL), lambda i: (0, 0)))
    row = lambda n: pl.BlockSpec((tm, n), lambda i: (i, 0))
    hm = lambda n: pl.BlockSpec((n, tm, HEAD_DIM), lambda i: (0, i, 0))
    hmt = lambda n: pl.BlockSpec((n, HEAD_DIM, tm), lambda i: (0, 0, i))
    sds = jax.ShapeDtypeStruct
    out_shape = (
        sds((rows, Q_NSA), BF16),
        sds((N_HEADS_NSA, HEAD_DIM, rows), BF16),
        sds((rows, 512), F32),
        sds((rows, 256), F32),
        sds((N_KV_NSA, rows, HEAD_DIM), BF16),
        sds((N_KV_NSA, HEAD_DIM, rows), BF16),
        sds((N_KV_NSA, rows, HEAD_DIM), BF16),
        sds((N_KV_NSA, HEAD_DIM, rows), BF16),
        sds((rows, 2 * LANES), F32),
        sds((2 * LANES, rows), F32),
        sds((rows, Q_SB), BF16),
        sds((rows, 2 * Q_SB), F32),
        sds((rows, Q_SB), BF16),
        sds((rows, Q_SB), BF16),
        sds((rows, 2 * D_MODEL), BF16),
    )
    out_specs = (row(Q_NSA), hmt(8), row(512), row(256), hm(2), hmt(2), hm(2), hmt(2), row(2 * LANES),
                 pl.BlockSpec((2 * LANES, tm), lambda i: (0, i)),
                 row(Q_SB), row(2 * Q_SB), row(Q_SB), row(Q_SB), row(2 * D_MODEL))
    return pl.pallas_call(
        _proj_kernel,
        grid=(rows // tm,),
        in_specs=[row(D_MODEL), mod_spec, mod_spec,
                  pl.BlockSpec((1, D_MODEL), lambda i: (0, 0)),
                  pl.BlockSpec((D_MODEL, C_END), lambda i: (0, 0)),
                  row(LANES), row(LANES)],
        out_specs=out_specs,
        out_shape=out_shape,
        compiler_params=_cparams(("arbitrary",)),
        name="proj_in",
    )(x, shift, scale, ng, w, cos, sin)


def _compress_kernel(pt_ref, pages_hbm, w1_ref, bias_ref, w2_ref, b2_ref, kc_ref, vct_ref,
                     buf, stage, sem, hlo, hhi, *, n_pages, cp, token_minor):
    b = pl.program_id(0)
    nch = n_pages // cp
    nh = cp * (PAGE // CMP_STRIDE)

    def copies(page, slot, k):
        if token_minor:
            return [pltpu.make_async_copy(pages_hbm.at[0, page, pl.ds(0, 2)], stage.at[slot, k], sem.at[slot])]
        return [pltpu.make_async_copy(pages_hbm.at[page, :, pl.ds(c * LANES, LANES)],
                                      buf.at[slot, c, pl.ds(k * PAGE, PAGE)], sem.at[slot]) for c in range(2)]

    def start(ch, slot):
        for k in range(cp):
            for cpy in copies(pt_ref[b * n_pages + ch * cp + k], slot, k):
                cpy.start()

    def wait(slot):
        for k in range(cp):
            for cpy in copies(0, slot, k):
                cpy.wait()

    def to_token_major(slot):
        def one_page(k, _):
            for c in range(2):
                m = stage[slot, k, c].reshape(N_KV_NSA * HEAD_DIM, PAGE)
                buf[slot, c, pl.ds(pl.multiple_of(k * PAGE, PAGE), PAGE), :] = m.T
            return 0
        lax.fori_loop(0, cp, one_page, 0)

    start(0, 0)
    for ch in range(nch):
        slot = ch % 2
        if ch + 1 < nch:
            start(ch + 1, 1 - slot)
        wait(slot)
        if token_minor:
            to_token_major(slot)
        for cg in range(4):
            c, g = cg // N_KV_NSA, cg % N_KV_NSA
            acc = jnp.zeros((nh, 2 * CMP_HIDDEN), F32)
            for p in range(CMP_STRIDE):
                xp = buf[slot, c, pl.ds(p, nh, stride=CMP_STRIDE), :]
                acc = acc + _dot(xp[:, g * HEAD_DIM:(g + 1) * HEAD_DIM].astype(BF16), w1_ref[c, p])
            hlo[pl.ds(ch * nh, nh), cg * CMP_HIDDEN:(cg + 1) * CMP_HIDDEN] = acc[:, :CMP_HIDDEN]
            hhi[pl.ds(ch * nh, nh), cg * CMP_HIDDEN:(cg + 1) * CMP_HIDDEN] = acc[:, CMP_HIDDEN:]
    n_half = nch * nh
    hhi[pl.ds(n_half, SUBLANES), :] = jnp.zeros((SUBLANES, 4 * CMP_HIDDEN), F32)
    pre = hlo[...] + hhi[pl.ds(1, n_half), :] + bias_ref[...]
    act = jax.nn.gelu(pre).astype(BF16)
    out = _dot(act, w2_ref[...]) + b2_ref[...]
    out_t = out.T
    for g in range(N_KV_NSA):
        kc_ref[g] = out[:, g * HEAD_DIM:(g + 1) * HEAD_DIM].astype(BF16)
        vct_ref[g] = out_t[(2 + g) * HEAD_DIM:(3 + g) * HEAD_DIM, :].astype(BF16)


def _compress_call(page_table, pages, w1, bias, w2, b2):
    nb, n_pages = page_table.shape
    cp = min(32, n_pages)
    assert n_pages % cp == 0
    n_half = n_pages * (PAGE // CMP_STRIDE)
    token_minor = pages.ndim == 6
    stage_shape = (2, cp, 2, N_KV_NSA, HEAD_DIM, PAGE) if token_minor else (SUBLANES, LANES)
    kern = functools.partial(_compress_kernel, n_pages=n_pages, cp=cp, token_minor=token_minor)
    const = lambda shape: pl.BlockSpec(shape, lambda b, pt: (0,) * len(shape))
    grid_spec = pltpu.PrefetchScalarGridSpec(
        num_scalar_prefetch=1,
        grid=(nb,),
        in_specs=[pl.BlockSpec(memory_space=pl.ANY),
                  const((2, CMP_STRIDE, HEAD_DIM, 2 * CMP_HIDDEN)),
                  const((1, 4 * CMP_HIDDEN)),
                  const((4 * CMP_HIDDEN, 4 * HEAD_DIM)),
                  const((1, 4 * HEAD_DIM))],
        out_specs=(pl.BlockSpec((None, N_KV_NSA, n_half, HEAD_DIM), lambda b, pt: (b, 0, 0, 0)),
                   pl.BlockSpec((None, N_KV_NSA, HEAD_DIM, n_half), lambda b, pt: (b, 0, 0, 0))),
        scratch_shapes=[pltpu.VMEM((2, 2, cp * PAGE, LANES), F32),
                        pltpu.VMEM(stage_shape, F32),
                        pltpu.SemaphoreType.DMA((2,)),
                        pltpu.VMEM((n_half, 4 * CMP_HIDDEN), F32),
                        pltpu.VMEM((n_half + SUBLANES, 4 * CMP_HIDDEN), F32)],
    )
    return pl.pallas_call(
        kern, grid_spec=grid_spec,
        out_shape=(jax.ShapeDtypeStruct((nb, N_KV_NSA, n_half, HEAD_DIM), BF16),
                   jax.ShapeDtypeStruct((nb, N_KV_NSA, HEAD_DIM, n_half), BF16)),
        compiler_params=_cparams(("arbitrary",)),
        name="compress_kv",
    )(page_table.reshape(-1), pages, w1, bias, w2, b2)


def _compress_weights(cmp_pe, cmp_w1, cmp_b1, cmp_w2, cmp_b2):
    w1r = cmp_w1.reshape(2, CMP_LEN, HEAD_DIM, CMP_HIDDEN)
    w1 = jnp.concatenate([w1r[:, :CMP_STRIDE], w1r[:, CMP_STRIDE:]], axis=-1).astype(BF16)
    bias = jnp.einsum('cpd,cpdh->ch', cmp_pe, w1r, precision=lax.Precision.HIGHEST) + cmp_b1
    bias = jnp.repeat(bias, N_KV_NSA, axis=0).reshape(1, 4 * CMP_HIDDEN)
    w2 = jnp.zeros((4, CMP_HIDDEN, 4, HEAD_DIM), F32)
    for cg in range(4):
        w2 = w2.at[cg, :, cg, :].set(cmp_w2[cg // 2])
    w2 = w2.reshape(4 * CMP_HIDDEN, 4 * HEAD_DIM).astype(BF16)
    b2 = jnp.repeat(cmp_b2, N_KV_NSA, axis=0).reshape(1, 4 * HEAD_DIM)
    return w1, bias, w2, b2


def _masked_softmax(s, mask, axis=-1):
    s = jnp.where(mask, s, NEG_INF)
    m = jnp.max(s, axis=axis, keepdims=True)
    e = jnp.where(mask, jnp.exp(s - m), 0.0)
    return e / jnp.maximum(jnp.sum(e, axis=axis, keepdims=True), TINY)


def _tap_matrix(n_cmp_pad, n_sel_pad, axis):
    shape = (n_cmp_pad, n_sel_pad) if axis == 1 else (n_sel_pad, n_cmp_pad)
    n = lax.broadcasted_iota(I32, shape, 1 - axis)
    j = lax.broadcasted_iota(I32, shape, axis)
    front = CMP_LEN // CMP_STRIDE - 1
    return jnp.where((n >= SEL_RATIO * j - front) & (n <= SEL_RATIO * j + SEL_RATIO - 1), 1.0, 0.0).astype(BF16)


def _importance(p_sum, n_sel_pad, axis):
    taps = _tap_matrix(p_sum.shape[axis], n_sel_pad, axis)
    p1, p2, p3 = _split3(p_sum)
    if axis == 1:
        return _dot(p1, taps) + _dot(p2, taps) + _dot(p3, taps)
    return _dot(taps, p1) + _dot(taps, p2) + _dot(taps, p3)


def _topk_rounds(score, k, axis):
    n = score.shape[axis]
    blk = lax.broadcasted_iota(I32, score.shape, axis).astype(F32)
    outs = []
    work = score
    for _ in range(k):
        mx = jnp.max(work, axis=axis, keepdims=True)
        idx = jnp.min(jnp.where(work == mx, blk, float(n)), axis=axis, keepdims=True)
        outs.append((mx, idx))
        work = jnp.where(blk == idx, -jnp.inf, work)
    return outs


def _block_scores(imp, q_pos, n_sel, axis):
    blk = lax.broadcasted_iota(I32, imp.shape, axis)
    cur = q_pos // SEL_BLOCK
    forced = (blk == 0) | (blk == cur) | (blk == cur - 1)
    score = jnp.where(forced, BIG, jnp.where(blk <= cur, imp, NEG_INF))
    return jnp.where(blk < n_sel, score, -jnp.inf)


def _nsa_prompt_kernel(qt_ref, kc_ref, vct_ref, ks_ref, vst_ref, kw_ref, vwt_ref, gatet_ref, o_ref, bias_scr,
                       *, qb, kb, n_cmp, n_sel, topk):
    i = pl.program_id(1)
    n_cmp_pad = kc_ref.shape[0]
    nq = GROUP_NSA * qb
    qt = jnp.concatenate([qt_ref[r] for r in range(GROUP_NSA)], axis=1)
    q_pos1 = i * qb + lax.broadcasted_iota(I32, (1, qb), 1)
    q_pos = jnp.concatenate([q_pos1] * GROUP_NSA, axis=1)

    s = _dot(kc_ref[...], qt)
    n = lax.broadcasted_iota(I32, (n_cmp_pad, nq), 0)
    p_c = _masked_softmax(s, (n * CMP_STRIDE + CMP_LEN - 1 <= q_pos) & (n < n_cmp), axis=0)
    o_c = _dot(vct_ref[...], p_c.astype(BF16))

    p_sum = p_c[:, 0:qb]
    for r in range(1, GROUP_NSA):
        p_sum = p_sum + p_c[:, r * qb:(r + 1) * qb]
    imp = _importance(p_sum, n_sel, axis=0)
    score = _block_scores(imp, q_pos1, n_sel, axis=0)
    blk = lax.broadcasted_iota(I32, (n_sel, qb), 0).astype(F32)
    bias = jnp.full((n_sel, qb), NEG_INF, F32)
    for val, idx in _topk_rounds(score, topk, axis=0):
        bias = jnp.where((blk == idx) & (val > 0.5 * NEG_INF), 0.0, bias)
    bias_scr[...] = bias

    bpk = kb // SEL_BLOCK
    nkb = (i * qb + qb + kb - 1) // kb

    def sel_step(j, carry, causal):
        m, l, acc = carry
        start = pl.multiple_of(j * kb, kb)
        s = _dot(ks_ref[pl.ds(start, kb), :], qt)
        b8 = bias_scr[pl.ds(pl.multiple_of(j * bpk, bpk), bpk), :]
        b1 = jnp.broadcast_to(b8[:, None, :], (bpk, SEL_BLOCK, qb)).reshape(kb, qb)
        s = s + jnp.concatenate([b1] * GROUP_NSA, axis=1)
        if causal:
            k_pos = start + lax.broadcasted_iota(I32, (kb, nq), 0)
            s = jnp.where(k_pos <= q_pos, s, NEG_INF)
        m_new = jnp.maximum(m, jnp.max(s, axis=0, keepdims=True))
        e = jnp.exp(s - m_new)
        alpha = jnp.exp(m - m_new)
        l = alpha * l + jnp.sum(e, axis=0, keepdims=True)
        acc = alpha * acc + _dot(vst_ref[:, pl.ds(start, kb)], e.astype(BF16))
        return m_new, l, acc

    carry = (jnp.full((1, nq), NEG_INF, F32), jnp.zeros((1, nq), F32), jnp.zeros((HEAD_DIM, nq), F32))
    npair = (nkb - 1) // 2
    carry = lax.fori_loop(0, npair, lambda j, c: sel_step(2 * j + 1, sel_step(2 * j, c, False), False), carry)
    carry = lax.cond(nkb - 1 > 2 * npair, lambda c: sel_step(nkb - 2, c, False), lambda c: c, carry)
    _, l_s, acc_s = sel_step(nkb - 1, carry, True)
    o_s = acc_s / jnp.maximum(l_s, TINY)

    wlen = WINDOW + qb
    wstart = pl.multiple_of(jnp.maximum(i * qb - WINDOW, 0), qb)
    s = _dot(kw_ref[pl.ds(wstart, wlen), :], qt)
    dist = q_pos - (wstart + lax.broadcasted_iota(I32, (wlen, nq), 0))
    p_w = _masked_softmax(s, (dist >= 0) & (dist < WINDOW), axis=0)
    o_w = _dot(vwt_ref[:, pl.ds(wstart, wlen)], p_w.astype(BF16))

    gt = gatet_ref[...]
    gate = lambda br: jnp.concatenate([gt[3 * r + br:3 * r + br + 1, :] for r in range(GROUP_NSA)], axis=1)
    o = gate(0) * o_c + gate(1) * o_s + gate(2) * o_w
    o = jnp.concatenate([o[:, r * qb:(r + 1) * qb] for r in range(GROUP_NSA)], axis=0)
    o_ref[...] = o.T.astype(o_ref.dtype)


def _nsa_prompt_call(qat, kcmp, vcmpt, kslc, vslct, kwin, vwint, gatest, n_cmp):
    t = qat.shape[2]
    qb, kb = 128, 512
    assert t % kb == 0 and t >= WINDOW + qb
    n_sel = t // SEL_BLOCK
    n_cmp_pad = kcmp.shape[1]
    kern = functools.partial(_nsa_prompt_kernel, qb=qb, kb=kb, n_cmp=n_cmp, n_sel=n_sel, topk=min(TOP_N, n_sel))
    rows = lambda n: pl.BlockSpec((None, n, HEAD_DIM), lambda g, i: (g, 0, 0))
    cols = lambda n: pl.BlockSpec((None, HEAD_DIM, n), lambda g, i: (g, 0, 0))
    return pl.pallas_call(
        kern,
        grid=(N_KV_NSA, t // qb),
        in_specs=[pl.BlockSpec((GROUP_NSA, HEAD_DIM, qb), lambda g, i: (g, 0, i)),
                  rows(n_cmp_pad), cols(n_cmp_pad), rows(t), cols(t), rows(t), cols(t),
                  pl.BlockSpec((LANES, qb), lambda g, i: (g, i))],
        out_specs=pl.BlockSpec((qb, QG), lambda g, i: (i, g)),
        out_shape=jax.ShapeDtypeStruct((t, Q_NSA), BF16),
        scratch_shapes=[pltpu.VMEM((n_sel, qb), F32)],
        compiler_params=_cparams(("arbitrary", "arbitrary")),
        name="nsa_prompt",
    )(qat, kcmp, vcmpt, kslc, vslct, kwin, vwint, gatest)


def _sb_weights(z, mask, carry):
    kb = z.shape[1]
    l1 = -_softplus(z)
    if mask is not None:
        l1 = jnp.where(mask, l1, 0.0)
    sp = lax.broadcasted_iota(I32, (kb, kb), 0)
    sc = lax.broadcasted_iota(I32, (kb, kb), 1)
    later = jnp.where(sp > sc, 1.0, 0.0).astype(BF16)
    hi = l1.astype(BF16)
    lo = (l1 - hi.astype(F32)).astype(BF16)
    after = _dot(hi, later) + _dot(lo, later) + carry
    a = jnp.exp(z + l1 + after)
    if mask is not None:
        a = jnp.where(mask, a, 0.0)
    return a, carry + jnp.sum(l1, axis=-1, keepdims=True)


def _sb_prompt_kernel(q_ref, k_ref, v_ref, o_ref, *, qb, kb, nh):
    i = pl.program_id(1)
    width = nh * HEAD_DIM
    q = q_ref[...]
    own = [lax.broadcasted_iota(I32, (qb, width), 1) // HEAD_DIM == h for h in range(nh)]
    qh = [jnp.where(own[h], q, jnp.zeros_like(q)) for h in range(nh)]
    q_pos = i * qb + lax.broadcasted_iota(I32, (qb, 1), 0)

    def cond(st):
        return (st[0] >= 0) & (st[3] > SB_EXIT)

    def body(st):
        j, carries, acc, _ = st
        start = pl.multiple_of(j * kb, kb)
        k = k_ref[pl.ds(start, kb), :]
        v = v_ref[pl.ds(start, kb), :]
        mask = (start + lax.broadcasted_iota(I32, (qb, kb), 1)) < q_pos
        new_carries = []
        cmax = None
        for h in range(nh):
            a, c = _sb_weights(_dot_nt(qh[h], k), mask, carries[h])
            acc = acc + jnp.where(own[h], _dot(a.astype(BF16), v), 0.0)
            new_carries.append(c)
            cmax = c if cmax is None else jnp.maximum(cmax, c)
        return j - 1, tuple(new_carries), acc, jnp.max(cmax)

    j0 = (i * qb + qb - 1) // kb
    st = (j0, tuple(jnp.zeros((qb, 1), F32) for _ in range(nh)), jnp.zeros((qb, width), F32), jnp.float32(0.0))
    acc = lax.while_loop(cond, body, st)[2]
    o_ref[...] = acc.astype(o_ref.dtype)


def _sb_prompt_call(qb_all, ksb, vsb):
    t = qb_all.shape[0]
    qb, kb, nh = 128, 256, 4
    assert t % kb == 0
    width = nh * HEAD_DIM
    kern = functools.partial(_sb_prompt_kernel, qb=qb, kb=kb, nh=nh)
    whole = pl.BlockSpec((t, width), lambda h, i: (0, h))
    blk = pl.BlockSpec((qb, width), lambda h, i: (i, h))
    return pl.pallas_call(
        kern,
        grid=(N_HEADS_SB // nh, t // qb),
        in_specs=[blk, whole, whole],
        out_specs=blk,
        out_shape=jax.ShapeDtypeStruct((t, Q_SB), BF16),
        compiler_params=_cparams(("arbitrary", "arbitrary")),
        name="sb_prompt",
    )(qb_all, ksb, vsb)


def _merge_kernel(oa_ref, ob_ref, mg_ref, wa_ref, wb_ref, wo_ref, x_ref, gate_ref, ng1_ref, ng2_ref,
                  shift_ref, scale_ref, x1_ref, h2_ref):
    ya = _dot(oa_ref[...], wa_ref[...])
    yb = _dot(ob_ref[...], wb_ref[...])
    y = mg_ref[:, 0:D_MODEL].astype(F32) * ya + mg_ref[:, D_MODEL:2 * D_MODEL].astype(F32) * yb
    o = _dot(y.astype(BF16), wo_ref[...])
    x1 = x_ref[...] + gate_ref[...] * (_rms(o) * ng1_ref[...])
    x1_ref[...] = x1
    h2 = (_rms(x1) * ng2_ref[...]) * (1.0 + scale_ref[...]) + shift_ref[...]
    h2_ref[...] = h2.astype(BF16)


def _merge_call(oa, ob, mg, wa, wb, wo, x, gate, ng1, ng2, shift, scale, tm):
    rows = x.shape[0]
    per_row = gate.shape[0] != 1
    mod_spec = (pl.BlockSpec((tm, D_MODEL), lambda i: (i, 0)) if per_row
                else pl.BlockSpec((1, D_MODEL), lambda i: (0, 0)))
    vec = pl.BlockSpec((1, D_MODEL), lambda i: (0, 0))
    row = lambda n: pl.BlockSpec((tm, n), lambda i: (i, 0))
    wspec = lambda n: pl.BlockSpec((n, D_MODEL), lambda i: (0, 0))
    return pl.pallas_call(
        _merge_kernel,
        grid=(rows // tm,),
        in_specs=[row(Q_NSA), row(Q_SB), row(2 * D_MODEL), wspec(Q_NSA), wspec(Q_SB), wspec(D_MODEL),
                  row(D_MODEL), mod_spec, vec, vec, mod_spec, mod_spec],
        out_specs=(row(D_MODEL), row(D_MODEL)),
        out_shape=(jax.ShapeDtypeStruct((rows, D_MODEL), F32), jax.ShapeDtypeStruct((rows, D_MODEL), BF16)),
        compiler_params=_cparams(("arbitrary",)),
        name="merge_out",
    )(oa, ob, mg, wa, wb, wo, x, gate, ng1, ng2, shift, scale)


def _ffn_kernel(h_ref, wg_ref, wu_ref, wd_ref, x1_ref, gate_ref, ng_ref, o_ref, acc_ref):
    k = pl.program_id(1)

    @pl.when(k == 0)
    def _():
        acc_ref[...] = jnp.zeros_like(acc_ref)

    h = h_ref[...]
    g = _dot(h, wg_ref[...])
    u = _dot(h, wu_ref[...])
    act = (g * jax.nn.sigmoid(g)) * u
    acc_ref[...] += _dot(act.astype(BF16), wd_ref[...])

    @pl.when(k == pl.num_programs(1) - 1)
    def _():
        o_ref[...] = x1_ref[...] + gate_ref[...] * (_rms(acc_ref[...]) * ng_ref[...])


def _ffn_call(h2, w_up, w_down, x1, gate, ng, tm):
    rows = h2.shape[0]
    d_ff = w_down.shape[0]
    tk = 256
    nk = d_ff // tk
    assert d_ff % tk == 0
    per_row = gate.shape[0] != 1
    mod_spec = (pl.BlockSpec((tm, D_MODEL), lambda i, k: (i, 0)) if per_row
                else pl.BlockSpec((1, D_MODEL), lambda i, k: (0, 0)))
    row = pl.BlockSpec((tm, D_MODEL), lambda i, k: (i, 0))
    return pl.pallas_call(
        _ffn_kernel,
        grid=(rows // tm, nk),
        in_specs=[row,
                  pl.BlockSpec((D_MODEL, tk), lambda i, k: (0, k)),
                  pl.BlockSpec((D_MODEL, tk), lambda i, k: (0, nk + k)),
                  pl.BlockSpec((tk, D_MODEL), lambda i, k: (k, 0)),
                  row, mod_spec,
                  pl.BlockSpec((1, D_MODEL), lambda i, k: (0, 0))],
        out_specs=row,
        out_shape=jax.ShapeDtypeStruct((rows, D_MODEL), F32),
        scratch_shapes=[pltpu.VMEM((tm, D_MODEL), F32)],
        compiler_params=_cparams(("arbitrary", "arbitrary")),
        name="ffn",
    )(h2, w_up, w_up, w_down, x1, gate, ng)


def _nsa_s1_kernel(q_ref, kc_ref, vct_ref, oc_ref, idx_ref, *, nt, past_len, n_cmp, n_sel, topk):
    n_cmp_pad = kc_ref.shape[1]
    n_sel_pad = ((n_sel + LANES - 1) // LANES) * LANES
    rows = GROUP_NSA * nt
    q_pos = past_len + lax.broadcasted_iota(I32, (rows, 1), 0) % nt
    n = lax.broadcasted_iota(I32, (rows, n_cmp_pad), 1)
    mask = (n * CMP_STRIDE + CMP_LEN - 1 <= q_pos) & (n < n_cmp)
    probs = []
    for g in range(N_KV_NSA):
        q = q_ref[g * rows:(g + 1) * rows, :]
        p = _masked_softmax(_dot_nt(q, kc_ref[g]), mask)
        oc_ref[g * rows:(g + 1) * rows, :] = _dot_nt(p.astype(BF16), vct_ref[g])
        probs.append(p)
    p_all = jnp.concatenate(probs, axis=0)
    ro = lax.broadcasted_iota(I32, (N_KV_NSA * nt, N_KV_NSA * rows), 0)
    ri = lax.broadcasted_iota(I32, (N_KV_NSA * nt, N_KV_NSA * rows), 1)
    pick = jnp.where((ro // nt == ri // rows) & (ro % nt == ri % nt), 1.0, 0.0).astype(BF16)
    p1, p2, p3 = _split3(p_all)
    p_sum = _dot(pick, p1) + _dot(pick, p2) + _dot(pick, p3)
    imp = _importance(p_sum, n_sel_pad, axis=1)
    t_pos = past_len + lax.broadcasted_iota(I32, (N_KV_NSA * nt, 1), 0) % nt
    score = _block_scores(imp, t_pos, n_sel, axis=1)
    lane = lax.broadcasted_iota(I32, (N_KV_NSA * nt, LANES), 1)
    out = jnp.full((N_KV_NSA * nt, LANES), -1.0, F32)
    for r, (val, idx) in enumerate(_topk_rounds(score, topk, axis=1)):
        out = jnp.where(lane == r, jnp.where(val > 0.5 * NEG_INF, idx, -1.0), out)
    idx_ref[...] = out.astype(I32)


def _nsa_s1_call(q1, kcmp, vcmpt, nt, past_len, n_cmp, n_sel):
    nb = q1.shape[0]
    n_half = kcmp.shape[2]
    kern = functools.partial(_nsa_s1_kernel, nt=nt, past_len=past_len, n_cmp=n_cmp, n_sel=n_sel,
                             topk=min(TOP_N, n_sel))
    rows = N_HEADS_NSA * nt
    per_b = lambda *shape: pl.BlockSpec((None,) + shape, lambda b: (b,) + (0,) * len(shape))
    return pl.pallas_call(
        kern,
        grid=(nb,),
        in_specs=[per_b(rows, HEAD_DIM), per_b(N_KV_NSA, n_half, HEAD_DIM), per_b(N_KV_NSA, HEAD_DIM, n_half)],
        out_specs=(per_b(rows, HEAD_DIM), per_b(N_KV_NSA * nt, LANES)),
        out_shape=(jax.ShapeDtypeStruct((nb, rows, HEAD_DIM), F32),
                   jax.ShapeDtypeStruct((nb, N_KV_NSA * nt, LANES), I32)),
        compiler_params=_cparams(("arbitrary",)),
        name="nsa_sample_topk",
    )(q1, kcmp, vcmpt)


def _nsa_s2_kernel(idx_s, pt_s, q_ref, idxv_ref, oc_ref, gate_ref, newkv_ref, wkv_ref, wnew_ref, cache_hbm,
                   o_ref, kbuf, vbuf, sem, os_scr, *, nt, n_pages, n_past_blocks, topk):
    b = pl.program_id(0)
    ngt = N_KV_NSA * nt
    bpp = PAGE // SEL_BLOCK

    def slot_copies(j, gt, s):
        page = pt_s[b * n_pages + j // bpp]
        dst = lambda buf: buf.at[gt, :, pl.ds(pl.multiple_of(s * PAGE, PAGE), PAGE)]
        return (pltpu.make_async_copy(cache_hbm.at[0, page, 2, gt // nt], dst(kbuf), sem.at[0]),
                pltpu.make_async_copy(cache_hbm.at[0, page, 3, gt // nt], dst(vbuf), sem.at[0]))

    def issue(n, _):
        gt, s = n // topk, n % topk
        j = idx_s[(b * ngt + gt) * topk + s]
        valid = (j >= 0) & (j < n_past_blocks)

        @pl.when(valid)
        def _():
            for cpy in slot_copies(j, gt, s):
                cpy.start()

        @pl.when(jnp.logical_not(valid))
        def _():
            zeros = jnp.zeros((HEAD_DIM, PAGE), F32)
            kbuf[gt, :, pl.ds(pl.multiple_of(s * PAGE, PAGE), PAGE)] = zeros
            vbuf[gt, :, pl.ds(pl.multiple_of(s * PAGE, PAGE), PAGE)] = zeros
        return 0

    def drain(n, _):
        gt, s = n // topk, n % topk
        j = idx_s[(b * ngt + gt) * topk + s]

        @pl.when((j >= 0) & (j < n_past_blocks))
        def _():
            for cpy in slot_copies(j, gt, s):
                cpy.wait()
        return 0

    lax.fori_loop(0, ngt * topk, issue, 0)

    rows = GROUP_NSA * nt
    t_row = lax.broadcasted_iota(I32, (rows, 1), 0) // GROUP_NSA
    nw = wkv_ref.shape[1]
    nn = newkv_ref.shape[1]
    new_ok = lax.broadcasted_iota(I32, (rows, nn), 1) <= t_row
    win_ok = lax.broadcasted_iota(I32, (rows, nw), 1) > t_row
    o_w = []
    for g in range(N_KV_NSA):
        q = q_ref[g * rows:(g + 1) * rows, :]
        s_old = jnp.where(win_ok, _dot_nt(q, wkv_ref[g]), NEG_INF)
        s_new = jnp.where(new_ok, _dot_nt(q, wnew_ref[g]), NEG_INF)
        m = jnp.maximum(jnp.max(s_old, axis=-1, keepdims=True), jnp.max(s_new, axis=-1, keepdims=True))
        e_old = jnp.where(win_ok, jnp.exp(s_old - m), 0.0)
        e_new = jnp.where(new_ok, jnp.exp(s_new - m), 0.0)
        l = jnp.sum(e_old, axis=-1, keepdims=True) + jnp.sum(e_new, axis=-1, keepdims=True)
        pv = _dot(e_old.astype(BF16), wkv_ref[g]) + _dot(e_new.astype(BF16), wnew_ref[g])
        o_w.append(pv[:, HEAD_DIM:] / jnp.maximum(l, TINY))
    o_w = jnp.concatenate(o_w, axis=0)

    lax.fori_loop(0, ngt * topk, drain, 0)

    idxv = idxv_ref[...]
    slot_valid = (idxv >= 0) & (idxv < n_past_blocks)
    odd = (idxv & (bpp - 1)) == 1
    es = lax.broadcasted_iota(I32, (LANES, topk * PAGE), 0)
    ek = lax.broadcasted_iota(I32, (LANES, topk * PAGE), 1)
    in_slot = es == ek // PAGE
    key_odd = (ek // SEL_BLOCK) % bpp == 1
    pick = lambda cond: jnp.where(cond, 1.0, 0.0).astype(BF16)
    key_valid = (_dot(pick(slot_valid & jnp.logical_not(odd)), pick(in_slot & jnp.logical_not(key_odd)))
                 + _dot(pick(slot_valid & odd), pick(in_slot & key_odd))) > 0.5
    for gt in range(ngt):
        g, t = gt // nt, gt % nt
        q = q_ref[gt * GROUP_NSA:(gt + 1) * GROUP_NSA, :]
        kvn = newkv_ref[g]
        ok_old = key_valid[gt:gt + 1, :]
        ok_new = lax.broadcasted_iota(I32, (GROUP_NSA, nn), 1) <= t
        s_old = jnp.where(ok_old, _dot(q[:, :HEAD_DIM], kbuf[gt].astype(BF16)), NEG_INF)
        s_new = jnp.where(ok_new, _dot_nt(q, kvn), NEG_INF)
        m = jnp.maximum(jnp.max(s_old, axis=-1, keepdims=True), jnp.max(s_new, axis=-1, keepdims=True))
        e_old = jnp.where(ok_old, jnp.exp(s_old - m), 0.0)
        e_new = jnp.where(ok_new, jnp.exp(s_new - m), 0.0)
        l = jnp.sum(e_old, axis=-1, keepdims=True) + jnp.sum(e_new, axis=-1, keepdims=True)
        pv = _dot_nt(e_old.astype(BF16), vbuf[gt].astype(BF16)) + _dot(e_new.astype(BF16), kvn)[:, HEAD_DIM:]
        os_scr[gt * GROUP_NSA:(gt + 1) * GROUP_NSA, :] = pv / jnp.maximum(l, TINY)

    gates = gate_ref[...]
    o_ref[...] = gates[:, 0:1] * oc_ref[...] + gates[:, 1:2] * os_scr[...] + gates[:, 2:3] * o_w


def _nsa_s2_call(idx, page_table, q2, oc, gates, newkv, wkv, wnew, cache, nt, past_len):
    nb, n_pages = page_table.shape
    topk = TOP_N
    ngt = N_KV_NSA * nt
    rows = N_HEADS_NSA * nt
    kern = functools.partial(_nsa_s2_kernel, nt=nt, n_pages=n_pages, n_past_blocks=past_len // SEL_BLOCK, topk=topk)
    per_b = lambda *shape: pl.BlockSpec((None,) + shape, lambda b, i_s, p_s: (b,) + (0,) * len(shape))
    grid_spec = pltpu.PrefetchScalarGridSpec(
        num_scalar_prefetch=2,
        grid=(nb,),
        in_specs=[per_b(rows, LANES), per_b(ngt, LANES), per_b(rows, HEAD_DIM), per_b(rows, LANES),
                  per_b(N_KV_NSA, newkv.shape[2], LANES), per_b(N_KV_NSA, wkv.shape[2], LANES),
                  per_b(N_KV_NSA, wnew.shape[2], LANES), pl.BlockSpec(memory_space=pl.ANY)],
        out_specs=per_b(rows, HEAD_DIM),
        scratch_shapes=[pltpu.VMEM((ngt, HEAD_DIM, topk * PAGE), F32),
                        pltpu.VMEM((ngt, HEAD_DIM, topk * PAGE), F32),
                        pltpu.SemaphoreType.DMA((1,)),
                        pltpu.VMEM((rows, HEAD_DIM), F32)],
    )
    return pl.pallas_call(
        kern, grid_spec=grid_spec,
        out_shape=jax.ShapeDtypeStruct((nb, rows, HEAD_DIM), F32),
        compiler_params=_cparams(("arbitrary",)),
        name="nsa_sample_gather",
    )(idx[:, :, :topk].reshape(-1), page_table.reshape(-1), q2, idx, oc, gates, newkv, wkv, wnew, cache)


def _sb_sample_kernel(pt_s, qbd_ref, newk_ref, newv_ref, cache_hbm, o_ref, buf, sem, *, nt, n_pages):
    b = pl.program_id(0)
    nh = N_HEADS_SB
    rows = nt * nh
    qbd = qbd_ref[...]
    own_head = (lax.broadcasted_iota(I32, (rows, Q_SB), 0) % nh
                == lax.broadcasted_iota(I32, (rows, Q_SB), 1) // HEAD_DIM)

    def step(kt, vt, mask, carry, acc):
        a, carry = _sb_weights(_dot(qbd, kt), mask, carry)
        return carry, acc + jnp.where(own_head, _dot_nt(a.astype(BF16), vt), 0.0)

    def copy(pg, slot):
        return pltpu.make_async_copy(cache_hbm.at[0, pt_s[b * n_pages + pg]], buf.at[slot], sem.at[slot])

    copy(n_pages - 1, 0).start()
    t_row = lax.broadcasted_iota(I32, (rows, PAGE), 0) // nh
    new_mask = lax.broadcasted_iota(I32, (rows, PAGE), 1) < t_row
    carry, acc = step(newk_ref[...], newv_ref[...], new_mask,
                      jnp.zeros((rows, 1), F32), jnp.zeros((rows, Q_SB), F32))

    def cond(st):
        return (st[0] >= 0) & (st[4] > SB_EXIT)

    def body(st):
        pg, slot, carry, acc, _ = st
        copy(pg, slot).wait()

        @pl.when(pg > 0)
        def _():
            copy(pg - 1, 1 - slot).start()

        kt = buf[slot, 0].reshape(Q_SB, PAGE).astype(BF16)
        vt = buf[slot, 1].reshape(Q_SB, PAGE).astype(BF16)
        carry, acc = step(kt, vt, None, carry, acc)
        return pg - 1, 1 - slot, carry, acc, jnp.max(carry)

    pg, slot, _, acc, _ = lax.while_loop(cond, body, (jnp.int32(n_pages - 1), jnp.int32(0), carry, acc, jnp.max(carry)))

    @pl.when(pg >= 0)
    def _():
        copy(pg, slot).wait()

    o_ref[...] = jnp.sum(acc.reshape(nt, nh, Q_SB), axis=1)


def _sb_sample_call(page_table, qbd, newk, newv, cache, nt):
    nb, n_pages = page_table.shape
    nh = N_HEADS_SB
    kern = functools.partial(_sb_sample_kernel, nt=nt, n_pages=n_pages)
    per_b = lambda *shape: pl.BlockSpec((None,) + shape, lambda b, p_s: (b,) + (0,) * len(shape))
    grid_spec = pltpu.PrefetchScalarGridSpec(
        num_scalar_prefetch=1,
        grid=(nb,),
        in_specs=[per_b(nt * nh, Q_SB), per_b(Q_SB, PAGE), per_b(Q_SB, PAGE), pl.BlockSpec(memory_space=pl.ANY)],
        out_specs=per_b(nt, Q_SB),
        scratch_shapes=[pltpu.VMEM((2, 2, nh, HEAD_DIM, PAGE), F32), pltpu.SemaphoreType.DMA((2,))],
    )
    return pl.pallas_call(
        kern, grid_spec=grid_spec,
        out_shape=jax.ShapeDtypeStruct((nb, nt, Q_SB), F32),
        compiler_params=_cparams(("arbitrary",)),
        name="sb_sample",
    )(page_table.reshape(-1), qbd, newk, newv, cache)


def _rope_tables(pos):
    half = ROPE_DIM // 2
    inv_freq = ROPE_THETA ** (-jnp.arange(half, dtype=F32) / half)
    ang = pos.astype(F32)[:, None] * inv_freq[None, :]
    cos, sin = jnp.cos(ang), jnp.sin(ang)
    one = jnp.ones((pos.shape[0], HEAD_DIM - ROPE_DIM), F32)
    c64 = jnp.concatenate([cos, cos, one], axis=1)
    s64 = jnp.concatenate([-sin, sin, jnp.zeros_like(one)], axis=1)
    return jnp.tile(c64, (1, LANES // HEAD_DIM)), jnp.tile(s64, (1, LANES // HEAD_DIM))


def _pad_w_in(w_in):
    c_ga = 512 + 768
    n_ga = 3 * N_HEADS_NSA
    ga = w_in[:, c_ga:c_ga + n_ga].reshape(D_MODEL, N_KV_NSA, n_ga // N_KV_NSA)
    ga = jnp.pad(ga, ((0, 0), (0, 0), (0, LANES - n_ga // N_KV_NSA))).reshape(D_MODEL, N_KV_NSA * LANES)
    return jnp.concatenate([w_in[:, :c_ga], ga, w_in[:, c_ga + n_ga:]], axis=1).astype(BF16)


def _pad_axis(x, axis, size):
    pad = [(0, 0)] * x.ndim
    pad[axis] = (0, size - x.shape[axis])
    return jnp.pad(x, pad)


def kernel(x_prompt, x_sample, cache_nsa_kv, cache_sb_kv, state_win_kv, page_table, c_prompt, c_sample, ada_w, ada_b, norm_g, w_in, cmp_pe, cmp_w1, cmp_b1, cmp_w2, cmp_b2, w_o_nsa, w_o_sb, w_out, ffn_w_up, ffn_w_down):
    assert ada_w.shape[0] == 1 and x_prompt.shape[0] == 1
    t = x_prompt.shape[1]
    nb, nt = x_sample.shape[:2]
    n_pages = page_table.shape[1]
    past_len = n_pages * PAGE
    assert nt <= 8 and state_win_kv.shape[2] == WINDOW and t >= WINDOW

    w = _pad_w_in(w_in[0])
    cw = _compress_weights(cmp_pe[0], cmp_w1[0], cmp_b1[0], cmp_w2[0], cmp_b2[0])
    wa = w_o_nsa[0].astype(BF16)
    wb = w_o_sb[0].astype(BF16)
    wo = w_out[0].astype(BF16)
    wup = ffn_w_up[0].astype(BF16)
    wdn = ffn_w_down[0].astype(BF16)
    ng = norm_g[0]

    c_all = _pad_axis(jnp.concatenate([c_prompt, c_sample], axis=0), 0, -(-(1 + nb) // SUBLANES) * SUBLANES)
    mod = _mod_call(c_all, ada_w[0], ada_b[0][None])
    modp = [mod[0:1, k * D_MODEL:(k + 1) * D_MODEL] for k in range(6)]
    mods = [jnp.repeat(mod[1:1 + nb, k * D_MODEL:(k + 1) * D_MODEL], nt, axis=0) for k in range(6)]

    xp = x_prompt[0]
    cos, sin = _rope_tables(jnp.arange(t))
    (_, qat, nsa_rows, win_rows, kslc, vslct, kwin, vwint, _, gatest, qb, sb_rows, ksb, vsb, mg) = _proj_call(
        xp, modp[0], modp[1], ng[0:1], w, cos, sin, tm=256)
    ident = jnp.arange(t // PAGE, dtype=I32).reshape(1, -1)
    kcmp, vcmpt = _compress_call(ident, nsa_rows.reshape(t // PAGE, PAGE, 512), *cw)
    oa = _nsa_prompt_call(qat, kcmp[0], vcmpt[0], kslc, vslct, kwin, vwint, gatest, n_cmp=t // CMP_STRIDE - 1)
    ob = _sb_prompt_call(qb, ksb, vsb)
    x1, h2 = _merge_call(oa, ob, mg, wa, wb, wo, xp, modp[2], ng[1:2], ng[2:3], modp[3], modp[4], tm=256)
    y_prompt = _ffn_call(h2, wup, wdn, x1, modp[5], ng[3:4], tm=512)

    rs = nb * nt
    xs = x_sample.reshape(rs, D_MODEL)
    cos_s, sin_s = _rope_tables(past_len + jnp.tile(jnp.arange(nt), nb))
    (qa_s, _, nsa_new, win_new, _, _, _, _, gates_s, _, qb_s, sb_new, _, _, mg_s) = _proj_call(
        xs, mods[0], mods[1], ng[0:1], w, cos_s, sin_s, tm=rs)

    nsa_cache = cache_nsa_kv.transpose(0, 1, 3, 4, 5, 2)
    sb_cache = cache_sb_kv.transpose(0, 1, 3, 4, 5, 2)
    kcmp_s, vcmpt_s = _compress_call(page_table, nsa_cache, *cw)
    n_cmp = past_len // CMP_STRIDE - 1
    n_sel = past_len // SEL_BLOCK + 1

    qa5 = qa_s.reshape(nb, nt, N_KV_NSA, GROUP_NSA, HEAD_DIM)
    q1 = qa5.transpose(0, 2, 3, 1, 4).reshape(nb, N_HEADS_NSA * nt, HEAD_DIM)
    oc, idx = _nsa_s1_call(q1, kcmp_s, vcmpt_s, nt, past_len, n_cmp, n_sel)
    q2 = qa5.transpose(0, 2, 1, 3, 4).reshape(nb, N_HEADS_NSA * nt, HEAD_DIM)
    q2 = _pad_axis(q2, 2, LANES)
    oc2 = oc.reshape(nb, N_KV_NSA, GROUP_NSA, nt, HEAD_DIM).transpose(0, 1, 3, 2, 4).reshape(nb, -1, HEAD_DIM)
    g2 = gates_s.reshape(nb, nt, N_KV_NSA, LANES)[..., :3 * GROUP_NSA].reshape(nb, nt, N_KV_NSA, GROUP_NSA, 3)
    g2 = _pad_axis(g2.transpose(0, 2, 1, 3, 4).reshape(nb, N_HEADS_NSA * nt, 3), 2, LANES)

    def new_rows_kv(rows_kv):
        kv = rows_kv.reshape(nb, nt, 2, N_KV_NSA, HEAD_DIM).transpose(0, 3, 1, 2, 4).reshape(nb, N_KV_NSA, nt, LANES)
        return _pad_axis(kv, 2, SUBLANES).astype(BF16)

    newkv = new_rows_kv(nsa_new[:, 256:512])
    wnew = new_rows_kv(win_new)
    wkv = state_win_kv[0].transpose(0, 3, 1, 2, 4).reshape(nb, N_KV_NSA, WINDOW, LANES).astype(BF16)
    oa_s = _nsa_s2_call(idx, page_table, q2, oc2, g2, newkv, wkv, wnew, nsa_cache, nt, past_len)
    oa_s = oa_s.reshape(nb, N_KV_NSA, nt, GROUP_NSA, HEAD_DIM).transpose(0, 2, 1, 3, 4).reshape(rs, Q_NSA).astype(BF16)

    qb4 = qb_s.reshape(nb, nt, N_HEADS_SB, 1, HEAD_DIM)
    eye = jnp.eye(N_HEADS_SB, dtype=BF16)[None, None, :, :, None]
    qbd = (qb4 * eye).reshape(nb, nt * N_HEADS_SB, Q_SB)
    new_sb = sb_new.reshape(nb, nt, 2, N_HEADS_SB, HEAD_DIM).transpose(2, 0, 3, 4, 1)
    new_sb = _pad_axis(new_sb, 4, PAGE).reshape(2, nb, Q_SB, PAGE).astype(BF16)
    ob_s = _sb_sample_call(page_table, qbd, new_sb[0], new_sb[1], sb_cache, nt)
    ob_s = ob_s.reshape(rs, Q_SB).astype(BF16)

    x1s, h2s = _merge_call(oa_s, ob_s, mg_s, wa, wb, wo, xs, mods[2], ng[1:2], ng[2:3], mods[3], mods[4], tm=rs)
    y_sample = _ffn_call(h2s, wup, wdn, x1s, mods[5], ng[3:4], tm=rs)

    win_new5 = win_new.reshape(nb, nt, 2, N_KV_NSA, HEAD_DIM)
    win_kv_sample = jnp.concatenate([state_win_kv[0], win_new5], axis=1)[:, nt:][:, -WINDOW:]
    return (y_prompt[None],
            y_sample.reshape(nb, nt, D_MODEL),
            nsa_rows.reshape(1, 1, t, 4, N_KV_NSA, HEAD_DIM),
            sb_rows.reshape(1, 1, t, 2, N_HEADS_SB, HEAD_DIM),
            win_rows[t - WINDOW:].reshape(1, 1, WINDOW, 2, N_KV_NSA, HEAD_DIM),
            nsa_new.reshape(1, nb, nt, 4, N_KV_NSA, HEAD_DIM),
            sb_new.reshape(1, nb, nt, 2, N_HEADS_SB, HEAD_DIM),
            win_kv_sample[None])
```

```python
import functools

import jax
import jax.numpy as jnp
from jax import lax
from jax.experimental import pallas as pl
from jax.experimental.pallas import tpu as pltpu

F32 = jnp.float32
BF16 = jnp.bfloat16
I32 = jnp.int32

D_MODEL = 1024
HEAD_DIM = 64
N_HEADS_NSA = 8
N_KV_NSA = 2
GROUP_NSA = N_HEADS_NSA // N_KV_NSA
N_HEADS_SB = 8
ROPE_DIM = HEAD_DIM // 4
ROPE_THETA = 500000.0
CMP_STRIDE = 16
CMP_LEN = 2 * CMP_STRIDE
CMP_HIDDEN = 4 * HEAD_DIM
SEL_BLOCK = 64
SEL_RATIO = SEL_BLOCK // CMP_STRIDE
TOP_N = 16
WINDOW = 512
PAGE = 128
RMS_EPS = 1e-6
NEG_INF = -1e30
BIG = 1e30
TINY = 1e-30
ATTN_SCALE = HEAD_DIM ** -0.5

LANES = 128
SUBLANES = 8
VMEM_LIMIT = 56 * 1024 * 1024
SB_EXIT = -104.0

C_QA = 0
C_KVA = 512
C_GA = 1280
C_QB = 1536
C_KVB = 2048
C_MG = 3072
C_END = 5120

Q_NSA = N_HEADS_NSA * HEAD_DIM
Q_SB = N_HEADS_SB * HEAD_DIM
QG = GROUP_NSA * HEAD_DIM


def _cparams(sem):
    return pltpu.CompilerParams(dimension_semantics=sem, vmem_limit_bytes=VMEM_LIMIT)


def _rms(x):
    return x * lax.rsqrt(jnp.mean(x * x, axis=-1, keepdims=True) + RMS_EPS)


def _dot(a, b):
    return jnp.dot(a, b, preferred_element_type=F32)


def _dot_nt(a, b):
    return lax.dot_general(a, b, (((1,), (1,)), ((), ())), preferred_element_type=F32)


def _split3(x):
    p1 = x.astype(BF16)
    r1 = x - p1.astype(F32)
    p2 = r1.astype(BF16)
    p3 = (r1 - p2.astype(F32)).astype(BF16)
    return p1, p2, p3


def _softplus(z):
    return jnp.maximum(z, 0.0) + jnp.log(1.0 + jnp.exp(-jnp.abs(z)))


def _mod_kernel(c_ref, w_ref, b_ref, o_ref):
    c = c_ref[...]
    s = c * jax.nn.sigmoid(c)
    o_ref[...] = _dot(s.astype(BF16), w_ref[...].astype(BF16)) + b_ref[...]


def _mod_call(c, w, b):
    rows, n = c.shape[0], w.shape[1]
    tn = 1536
    return pl.pallas_call(
        _mod_kernel,
        grid=(n // tn,),
        in_specs=[pl.BlockSpec((rows, D_MODEL), lambda j: (0, 0)),
                  pl.BlockSpec((D_MODEL, tn), lambda j: (0, j)),
                  pl.BlockSpec((1, tn), lambda j: (0, j))],
        out_specs=pl.BlockSpec((rows, tn), lambda j: (0, j)),
        out_shape=jax.ShapeDtypeStruct((rows, n), F32),
        compiler_params=_cparams(("arbitrary",)),
        name="adaln_mod",
    )(c, w, b)


def _rope(y, cos, sin):
    w = y.shape[1]
    reps = w // LANES
    if reps > 1:
        cos = jnp.concatenate([cos] * reps, axis=1)
        sin = jnp.concatenate([sin] * reps, axis=1)
    lane = lax.broadcasted_iota(I32, y.shape, 1) & (HEAD_DIM - 1)
    half = ROPE_DIM // 2
    partner = jnp.where(lane < half, pltpu.roll(y, w - half, 1), pltpu.roll(y, half, 1))
    return y * cos + partner * sin


def _proj_kernel(x_ref, shift_ref, scale_ref, ng_ref, w_ref, cos_ref, sin_ref,
                 qa_ref, qat_ref, nsa_ref, win_ref, kslc_ref, vslct_ref, kwin_ref, vwint_ref, gate_ref, gatet_ref,
                 qb_ref, sb_ref, ksb_ref, vsb_ref, mg_ref):
    x = x_ref[...]
    h = _rms(x) * ng_ref[...]
    h = h * (1.0 + scale_ref[...]) + shift_ref[...]
    hb = h.astype(BF16)
    cos = cos_ref[...]
    sin = sin_ref[...]

    def mm(c0, c1):
        return _dot(hb, w_ref[:, c0:c1])

    def heads(dst, y, n):
        for i in range(n):
            dst[i] = y[:, i * HEAD_DIM:(i + 1) * HEAD_DIM].astype(dst.dtype)

    def heads_t(dst, y, n):
        yt = y.T
        for i in range(n):
            dst[i] = yt[i * HEAD_DIM:(i + 1) * HEAD_DIM, :].astype(dst.dtype)

    qa = _rope(mm(C_QA, C_QA + Q_NSA), cos, sin) * ATTN_SCALE
    qa_ref[...] = qa.astype(BF16)
    heads_t(qat_ref, qa, N_HEADS_NSA)

    kc = _rope(mm(C_KVA, C_KVA + 128), cos, sin)
    vc = mm(C_KVA + 128, C_KVA + 256)
    ks = _rope(mm(C_KVA + 256, C_KVA + 384), cos, sin)
    vs = mm(C_KVA + 384, C_KVA + 512)
    kw = _rope(mm(C_KVA + 512, C_KVA + 640), cos, sin)
    vw = mm(C_KVA + 640, C_KVA + 768)
    nsa_ref[:, 0:128] = kc
    nsa_ref[:, 128:256] = vc
    nsa_ref[:, 256:384] = ks
    nsa_ref[:, 384:512] = vs
    win_ref[:, 0:128] = kw
    win_ref[:, 128:256] = vw
    heads(kslc_ref, ks, N_KV_NSA)
    heads_t(vslct_ref, vs, N_KV_NSA)
    heads(kwin_ref, kw, N_KV_NSA)
    heads_t(vwint_ref, vw, N_KV_NSA)

    gates = jax.nn.sigmoid(mm(C_GA, C_GA + 2 * LANES))
    gate_ref[...] = gates
    gatet_ref[...] = gates.T

    qb_ref[...] = (mm(C_QB, C_QB + Q_SB) * ATTN_SCALE).astype(BF16)

    kb = mm(C_KVB, C_KVB + Q_SB)
    vb = mm(C_KVB + Q_SB, C_KVB + 2 * Q_SB)
    sb_ref[:, 0:Q_SB] = kb
    sb_ref[:, Q_SB:2 * Q_SB] = vb
    ksb_ref[...] = kb.astype(BF16)
    vsb_ref[...] = vb.astype(BF16)

    for j in range(2):
        mg_ref[:, j * D_MODEL:(j + 1) * D_MODEL] = jax.nn.sigmoid(
            mm(C_MG + j * D_MODEL, C_MG + (j + 1) * D_MODEL)).astype(BF16)


def _proj_call(x, shift, scale, ng, w, cos, sin, tm):
    rows = x.shape[0]
    per_row = shift.shape[0] != 1
    mod_spec = (pl.BlockSpec((tm, D_MODEL), lambda i: (i, 0)) if per_row
                else pl.BlockSpec((1, D_MODEL), lambda i: (0, 0)))
    row = lambda n: pl.BlockSpec((tm, n), lambda i: (i, 0))
    hm = lambda n: pl.BlockSpec((n, tm, HEAD_DIM), lambda i: (0, i, 0))
    hmt = lambda n: pl.BlockSpec((n, HEAD_DIM, tm), lambda i: (0, 0, i))
    sds = jax.ShapeDtypeStruct
    out_shape = (
        sds((rows, Q_NSA), BF16),
        sds((N_HEADS_NSA, HEAD_DIM, rows), BF16),
        sds((rows, 512), F32),
        sds((rows, 256), F32),
        sds((N_KV_NSA, rows, HEAD_DIM), BF16),
        sds((N_KV_NSA, HEAD_DIM, rows), BF16),
        sds((N_KV_NSA, rows, HEAD_DIM), BF16),
        sds((N_KV_NSA, HEAD_DIM, rows), BF16),
        sds((rows, 2 * LANES), F32),
        sds((2 * LANES, rows), F32),
        sds((rows, Q_SB), BF16),
        sds((rows, 2 * Q_SB), F32),
        sds((rows, Q_SB), BF16),
        sds((rows, Q_SB), BF16),
        sds((rows, 2 * D_MODEL), BF16),
    )
    out_specs = (row(Q_NSA), hmt(8), row(512), row(256), hm(2), hmt(2), hm(2), hmt(2), row(2 * LANES),
                 pl.BlockSpec((2 * LANES, tm), lambda i: (0, i)),
                 row(Q_SB), row(2 * Q_SB), row(Q_SB), row(Q_SB), row(2 * D_MODEL))
    return pl.pallas_call(
        _proj_kernel,
        grid=(rows // tm,),
        in_specs=[row(D_MODEL), mod_spec, mod_spec,
                  pl.BlockSpec((1, D_MODEL), lambda i: (0, 0)),
                  pl.BlockSpec((D_MODEL, C_END), lambda i: (0, 0)),
                  row(LANES), row(LANES)],
        out_specs=out_specs,
        out_shape=out_shape,
        compiler_params=_cparams(("arbitrary",)),
        name="proj_in",
    )(x, shift, scale, ng, w, cos, sin)


def _compress_kernel(pt_ref, pages_hbm, w1_ref, bias_ref, w2_ref, b2_ref, kc_ref, vct_ref,
                     buf, stage, sem, hlo, hhi, *, n_pages, cp, token_minor):
    b = pl.program_id(0)
    nch = n_pages // cp
    nh = cp * (PAGE // CMP_STRIDE)

    def copies(page, slot, k):
        if token_minor:
            return [pltpu.make_async_copy(pages_hbm.at[0, page, pl.ds(0, 2)], stage.at[slot, k], sem.at[slot])]
        return [pltpu.make_async_copy(pages_hbm.at[page, :, pl.ds(c * LANES, LANES)],
                                      buf.at[slot, c, pl.ds(k * PAGE, PAGE)], sem.at[slot]) for c in range(2)]

    def start(ch, slot):
        for k in range(cp):
            for cpy in copies(pt_ref[b * n_pages + ch * cp + k], slot, k):
                cpy.start()

    def wait(slot):
        for k in range(cp):
            for cpy in copies(0, slot, k):
                cpy.wait()

    def to_token_major(slot):
        def one_page(k, _):
            for c in range(2):
                m = stage[slot, k, c].reshape(N_KV_NSA * HEAD_DIM, PAGE)
                buf[slot, c, pl.ds(pl.multiple_of(k * PAGE, PAGE), PAGE), :] = m.T
            return 0
        lax.fori_loop(0, cp, one_page, 0)

    start(0, 0)
    for ch in range(nch):
        slot = ch % 2
        if ch + 1 < nch:
            start(ch + 1, 1 - slot)
        wait(slot)
        if token_minor:
            to_token_major(slot)
        low = lax.broadcasted_iota(I32, (nh, LANES), 1) < HEAD_DIM

        def regroup(a, b_):
            return (jnp.where(low, a, pltpu.roll(b_, HEAD_DIM, 1)), jnp.where(low, pltpu.roll(a, HEAD_DIM, 1), b_))

        for c in range(2):
            accs = [jnp.zeros((nh, 2 * CMP_HIDDEN), F32) for _ in range(N_KV_NSA)]
            for pq in range(CMP_STRIDE // 4):
                xp = [buf[slot, c, pl.ds(4 * pq + j, nh, stride=CMP_STRIDE), :] for j in range(4)]
                ab, cd = regroup(xp[0], xp[1]), regroup(xp[2], xp[3])
                for g in range(N_KV_NSA):
                    x4 = jnp.concatenate([ab[g], cd[g]], axis=1).astype(BF16)
                    accs[g] = accs[g] + _dot(x4, w1_ref[c, pq])
            for g in range(N_KV_NSA):
                cg = c * N_KV_NSA + g
                hlo[pl.ds(ch * nh, nh), cg * CMP_HIDDEN:(cg + 1) * CMP_HIDDEN] = accs[g][:, :CMP_HIDDEN]
                hhi[pl.ds(ch * nh, nh), cg * CMP_HIDDEN:(cg + 1) * CMP_HIDDEN] = accs[g][:, CMP_HIDDEN:]
    n_half = nch * nh
    hhi[pl.ds(n_half, SUBLANES), :] = jnp.zeros((SUBLANES, 4 * CMP_HIDDEN), F32)
    pre = hlo[...] + hhi[pl.ds(1, n_half), :] + bias_ref[...]
    act = jax.nn.gelu(pre).astype(BF16)
    out = _dot(act, w2_ref[...]) + b2_ref[...]
    out_t = out.T
    for g in range(N_KV_NSA):
        kc_ref[g] = out[:, g * HEAD_DIM:(g + 1) * HEAD_DIM].astype(BF16)
        vct_ref[g] = out_t[(2 + g) * HEAD_DIM:(3 + g) * HEAD_DIM, :].astype(BF16)


def _compress_call(page_table, pages, w1, bias, w2, b2):
    nb, n_pages = page_table.shape
    cp = min(32, n_pages)
    assert n_pages % cp == 0
    n_half = n_pages * (PAGE // CMP_STRIDE)
    token_minor = pages.ndim == 6
    stage_shape = (2, cp, 2, N_KV_NSA, HEAD_DIM, PAGE) if token_minor else (SUBLANES, LANES)
    kern = functools.partial(_compress_kernel, n_pages=n_pages, cp=cp, token_minor=token_minor)
    const = lambda shape: pl.BlockSpec(shape, lambda b, pt: (0,) * len(shape))
    grid_spec = pltpu.PrefetchScalarGridSpec(
        num_scalar_prefetch=1,
        grid=(nb,),
        in_specs=[pl.BlockSpec(memory_space=pl.ANY),
                  const((2, CMP_STRIDE // 4, 4 * HEAD_DIM, 2 * CMP_HIDDEN)),
                  const((1, 4 * CMP_HIDDEN)),
                  const((4 * CMP_HIDDEN, 4 * HEAD_DIM)),
                  const((1, 4 * HEAD_DIM))],
        out_specs=(pl.BlockSpec((None, N_KV_NSA, n_half, HEAD_DIM), lambda b, pt: (b, 0, 0, 0)),
                   pl.BlockSpec((None, N_KV_NSA, HEAD_DIM, n_half), lambda b, pt: (b, 0, 0, 0))),
        scratch_shapes=[pltpu.VMEM((2, 2, cp * PAGE, LANES), F32),
                        pltpu.VMEM(stage_shape, F32),
                        pltpu.SemaphoreType.DMA((2,)),
                        pltpu.VMEM((n_half, 4 * CMP_HIDDEN), F32),
                        pltpu.VMEM((n_half + SUBLANES, 4 * CMP_HIDDEN), F32)],
    )
    return pl.pallas_call(
        kern, grid_spec=grid_spec,
        out_shape=(jax.ShapeDtypeStruct((nb, N_KV_NSA, n_half, HEAD_DIM), BF16),
                   jax.ShapeDtypeStruct((nb, N_KV_NSA, HEAD_DIM, n_half), BF16)),
        compiler_params=_cparams(("arbitrary",)),
        name="compress_kv",
    )(page_table.reshape(-1), pages, w1, bias, w2, b2)


def _compress_weights(cmp_pe, cmp_w1, cmp_b1, cmp_w2, cmp_b2):
    w1r = cmp_w1.reshape(2, CMP_LEN, HEAD_DIM, CMP_HIDDEN)
    w1 = jnp.concatenate([w1r[:, :CMP_STRIDE], w1r[:, CMP_STRIDE:]], axis=-1).astype(BF16)
    w1 = w1.reshape(2, CMP_STRIDE // 4, 4 * HEAD_DIM, 2 * CMP_HIDDEN)
    bias = jnp.einsum('cpd,cpdh->ch', cmp_pe, w1r, precision=lax.Precision.HIGHEST) + cmp_b1
    bias = jnp.repeat(bias, N_KV_NSA, axis=0).reshape(1, 4 * CMP_HIDDEN)
    w2 = jnp.zeros((4, CMP_HIDDEN, 4, HEAD_DIM), F32)
    for cg in range(4):
        w2 = w2.at[cg, :, cg, :].set(cmp_w2[cg // 2])
    w2 = w2.reshape(4 * CMP_HIDDEN, 4 * HEAD_DIM).astype(BF16)
    b2 = jnp.repeat(cmp_b2, N_KV_NSA, axis=0).reshape(1, 4 * HEAD_DIM)
    return w1, bias, w2, b2


def _masked_softmax(s, mask, axis=-1):
    s = jnp.where(mask, s, NEG_INF)
    m = jnp.max(s, axis=axis, keepdims=True)
    e = jnp.where(mask, jnp.exp(s - m), 0.0)
    return e / jnp.maximum(jnp.sum(e, axis=axis, keepdims=True), TINY)


def _tap_matrix(n_cmp_pad, n_sel_pad, axis):
    shape = (n_cmp_pad, n_sel_pad) if axis == 1 else (n_sel_pad, n_cmp_pad)
    n = lax.broadcasted_iota(I32, shape, 1 - axis)
    j = lax.broadcasted_iota(I32, shape, axis)
    front = CMP_LEN // CMP_STRIDE - 1
    return jnp.where((n >= SEL_RATIO * j - front) & (n <= SEL_RATIO * j + SEL_RATIO - 1), 1.0, 0.0).astype(BF16)


def _importance(p_sum, n_sel_pad, axis):
    taps = _tap_matrix(p_sum.shape[axis], n_sel_pad, axis)
    p1, p2, p3 = _split3(p_sum)
    if axis == 1:
        return _dot(p1, taps) + _dot(p2, taps) + _dot(p3, taps)
    return _dot(taps, p1) + _dot(taps, p2) + _dot(taps, p3)


def _topk_rounds(score, k, axis):
    n = score.shape[axis]
    blk = lax.broadcasted_iota(I32, score.shape, axis).astype(F32)
    outs = []
    work = score
    for _ in range(k):
        mx = jnp.max(work, axis=axis, keepdims=True)
        idx = jnp.min(jnp.where(work == mx, blk, float(n)), axis=axis, keepdims=True)
        outs.append((mx, idx))
        work = jnp.where(blk == idx, -jnp.inf, work)
    return outs


def _block_scores(imp, q_pos, n_sel, axis):
    blk = lax.broadcasted_iota(I32, imp.shape, axis)
    cur = q_pos // SEL_BLOCK
    forced = (blk == 0) | (blk == cur) | (blk == cur - 1)
    score = jnp.where(forced, BIG, jnp.where(blk <= cur, imp, NEG_INF))
    return jnp.where(blk < n_sel, score, -jnp.inf)


def _nsa_prompt_kernel(qt_ref, kc_ref, vct_ref, ks_ref, vst_ref, kw_ref, vwt_ref, gatet_ref, o_ref, bias_scr,
                       *, qb, kb, n_cmp, n_sel, topk):
    i = pl.program_id(1)
    n_cmp_pad = kc_ref.shape[0]
    nq = GROUP_NSA * qb
    qt = jnp.concatenate([qt_ref[r] for r in range(GROUP_NSA)], axis=1)
    q_pos1 = i * qb + lax.broadcasted_iota(I32, (1, qb), 1)
    q_pos = jnp.concatenate([q_pos1] * GROUP_NSA, axis=1)

    s = _dot(kc_ref[...], qt)
    n = lax.broadcasted_iota(I32, (n_cmp_pad, nq), 0)
    p_c = _masked_softmax(s, (n * CMP_STRIDE + CMP_LEN - 1 <= q_pos) & (n < n_cmp), axis=0)
    o_c = _dot(vct_ref[...], p_c.astype(BF16))

    p_sum = p_c[:, 0:qb]
    for r in range(1, GROUP_NSA):
        p_sum = p_sum + p_c[:, r * qb:(r + 1) * qb]
    imp = _importance(p_sum, n_sel, axis=0)
    score = _block_scores(imp, q_pos1, n_sel, axis=0)
    blk = lax.broadcasted_iota(I32, (n_sel, qb), 0).astype(F32)
    bias = jnp.full((n_sel, qb), NEG_INF, F32)
    for val, idx in _topk_rounds(score, topk, axis=0):
        bias = jnp.where((blk == idx) & (val > 0.5 * NEG_INF), 0.0, bias)
    bias_scr[...] = bias

    bpk = kb // SEL_BLOCK
    nkb = (i * qb + qb + kb - 1) // kb

    def sel_step(j, carry, causal):
        m, l, acc = carry
        start = pl.multiple_of(j * kb, kb)
        s = _dot(ks_ref[pl.ds(start, kb), :], qt)
        b8 = bias_scr[pl.ds(pl.multiple_of(j * bpk, bpk), bpk), :]
        b1 = jnp.broadcast_to(b8[:, None, :], (bpk, SEL_BLOCK, qb)).reshape(kb, qb)
        s = s + jnp.concatenate([b1] * GROUP_NSA, axis=1)
        if causal:
            k_pos = start + lax.broadcasted_iota(I32, (kb, nq), 0)
            s = jnp.where(k_pos <= q_pos, s, NEG_INF)
        m_new = jnp.maximum(m, jnp.max(s, axis=0, keepdims=True))
        e = jnp.exp(s - m_new)
        alpha = jnp.exp(m - m_new)
        l = alpha * l + jnp.sum(e, axis=0, keepdims=True)
        acc = alpha * acc + _dot(vst_ref[:, pl.ds(start, kb)], e.astype(BF16))
        return m_new, l, acc

    carry = (jnp.full((1, nq), NEG_INF, F32), jnp.zeros((1, nq), F32), jnp.zeros((HEAD_DIM, nq), F32))
    npair = (nkb - 1) // 2
    carry = lax.fori_loop(0, npair, lambda j, c: sel_step(2 * j + 1, sel_step(2 * j, c, False), False), carry)
    carry = lax.cond(nkb - 1 > 2 * npair, lambda c: sel_step(nkb - 2, c, False), lambda c: c, carry)
    _, l_s, acc_s = sel_step(nkb - 1, carry, True)
    o_s = acc_s / jnp.maximum(l_s, TINY)

    wlen = WINDOW + qb
    wstart = pl.multiple_of(jnp.maximum(i * qb - WINDOW, 0), qb)
    s = _dot(kw_ref[pl.ds(wstart, wlen), :], qt)
    dist = q_pos - (wstart + lax.broadcasted_iota(I32, (wlen, nq), 0))
    p_w = _masked_softmax(s, (dist >= 0) & (dist < WINDOW), axis=0)
    o_w = _dot(vwt_ref[:, pl.ds(wstart, wlen)], p_w.astype(BF16))

    gt = gatet_ref[...]
    gate = lambda br: jnp.concatenate([gt[3 * r + br:3 * r + br + 1, :] for r in range(GROUP_NSA)], axis=1)
    o = gate(0) * o_c + gate(1) * o_s + gate(2) * o_w
    o = jnp.concatenate([o[:, r * qb:(r + 1) * qb] for r in range(GROUP_NSA)], axis=0)
    o_ref[...] = o.T.astype(o_ref.dtype)


def _nsa_prompt_call(qat, kcmp, vcmpt, kslc, vslct, kwin, vwint, gatest, n_cmp):
    t = qat.shape[2]
    qb, kb = 128, 512
    assert t % kb == 0 and t >= WINDOW + qb
    n_sel = t // SEL_BLOCK
    n_cmp_pad = kcmp.shape[1]
    kern = functools.partial(_nsa_prompt_kernel, qb=qb, kb=kb, n_cmp=n_cmp, n_sel=n_sel, topk=min(TOP_N, n_sel))
    rows = lambda n: pl.BlockSpec((None, n, HEAD_DIM), lambda g, i: (g, 0, 0))
    cols = lambda n: pl.BlockSpec((None, HEAD_DIM, n), lambda g, i: (g, 0, 0))
    return pl.pallas_call(
        kern,
        grid=(N_KV_NSA, t // qb),
        in_specs=[pl.BlockSpec((GROUP_NSA, HEAD_DIM, qb), lambda g, i: (g, 0, i)),
                  rows(n_cmp_pad), cols(n_cmp_pad), rows(t), cols(t), rows(t), cols(t),
                  pl.BlockSpec((LANES, qb), lambda g, i: (g, i))],
        out_specs=pl.BlockSpec((qb, QG), lambda g, i: (i, g)),
        out_shape=jax.ShapeDtypeStruct((t, Q_NSA), BF16),
        scratch_shapes=[pltpu.VMEM((n_sel, qb), F32)],
        compiler_params=_cparams(("arbitrary", "arbitrary")),
        name="nsa_prompt",
    )(qat, kcmp, vcmpt, kslc, vslct, kwin, vwint, gatest)


def _sb_weights(z, mask, carry):
    kb = z.shape[1]
    l1 = -_softplus(z)
    if mask is not None:
        l1 = jnp.where(mask, l1, 0.0)
    sp = lax.broadcasted_iota(I32, (kb, kb), 0)
    sc = lax.broadcasted_iota(I32, (kb, kb), 1)
    later = jnp.where(sp > sc, 1.0, 0.0).astype(BF16)
    hi = l1.astype(BF16)
    lo = (l1 - hi.astype(F32)).astype(BF16)
    after = _dot(hi, later) + _dot(lo, later) + carry
    a = jnp.exp(z + l1 + after)
    if mask is not None:
        a = jnp.where(mask, a, 0.0)
    return a, carry + jnp.sum(l1, axis=-1, keepdims=True)


def _sb_prompt_kernel(q_ref, k_ref, v_ref, o_ref, *, qb, kb, nh):
    i = pl.program_id(1)
    width = nh * HEAD_DIM
    q = q_ref[...]
    own = [lax.broadcasted_iota(I32, (qb, width), 1) // HEAD_DIM == h for h in range(nh)]
    qh = [jnp.where(own[h], q, jnp.zeros_like(q)) for h in range(nh)]
    q_pos = i * qb + lax.broadcasted_iota(I32, (qb, 1), 0)

    def cond(st):
        return (st[0] > 0) & (st[3] > SB_EXIT)

    def body(st):
        hi, carries, acc, _ = st
        start = pl.multiple_of(jnp.maximum(hi - kb, 0), qb)
        k = k_ref[pl.ds(start, kb), :]
        v = v_ref[pl.ds(start, kb), :]
        mask = (start + lax.broadcasted_iota(I32, (qb, kb), 1)) < jnp.minimum(q_pos, hi)
        new_carries = []
        cmax = None
        for h in range(nh):
            a, c = _sb_weights(_dot_nt(qh[h], k), mask, carries[h])
            acc = acc + jnp.where(own[h], _dot(a.astype(BF16), v), 0.0)
            new_carries.append(c)
            cmax = c if cmax is None else jnp.maximum(cmax, c)
        return hi - kb, tuple(new_carries), acc, jnp.max(cmax)

    st = ((i + 1) * qb, tuple(jnp.zeros((qb, 1), F32) for _ in range(nh)), jnp.zeros((qb, width), F32),
          jnp.float32(0.0))
    acc = lax.while_loop(cond, body, st)[2]
    o_ref[...] = acc.astype(o_ref.dtype)


def _sb_prompt_call(qb_all, ksb, vsb):
    t = qb_all.shape[0]
    qb, kb, nh = 128, 384, 4
    assert t % qb == 0 and kb % qb == 0 and t >= kb
    width = nh * HEAD_DIM
    kern = functools.partial(_sb_prompt_kernel, qb=qb, kb=kb, nh=nh)
    whole = pl.BlockSpec((t, width), lambda h, i: (0, h))
    blk = pl.BlockSpec((qb, width), lambda h, i: (i, h))
    return pl.pallas_call(
        kern,
        grid=(N_HEADS_SB // nh, t // qb),
        in_specs=[blk, whole, whole],
        out_specs=blk,
        out_shape=jax.ShapeDtypeStruct((t, Q_SB), BF16),
        compiler_params=_cparams(("arbitrary", "arbitrary")),
        name="sb_prompt",
    )(qb_all, ksb, vsb)


def _merge_kernel(oa_ref, ob_ref, mg_ref, wa_ref, wb_ref, wo_ref, x_ref, gate_ref, ng1_ref, ng2_ref,
                  shift_ref, scale_ref, x1_ref, h2_ref):
    ya = _dot(oa_ref[...], wa_ref[...])
    yb = _dot(ob_ref[...], wb_ref[...])
    y = mg_ref[:, 0:D_MODEL].astype(F32) * ya + mg_ref[:, D_MODEL:2 * D_MODEL].astype(F32) * yb
    o = _dot(y.astype(BF16), wo_ref[...])
    x1 = x_ref[...] + gate_ref[...] * (_rms(o) * ng1_ref[...])
    x1_ref[...] = x1
    h2 = (_rms(x1) * ng2_ref[...]) * (1.0 + scale_ref[...]) + shift_ref[...]
    h2_ref[...] = h2.astype(BF16)


def _merge_call(oa, ob, mg, wa, wb, wo, x, gate, ng1, ng2, shift, scale, tm):
    rows = x.shape[0]
    per_row = gate.shape[0] != 1
    mod_spec = (pl.BlockSpec((tm, D_MODEL), lambda i: (i, 0)) if per_row
                else pl.BlockSpec((1, D_MODEL), lambda i: (0, 0)))
    vec = pl.BlockSpec((1, D_MODEL), lambda i: (0, 0))
    row = lambda n: pl.BlockSpec((tm, n), lambda i: (i, 0))
    wspec = lambda n: pl.BlockSpec((n, D_MODEL), lambda i: (0, 0))
    return pl.pallas_call(
        _merge_kernel,
        grid=(rows // tm,),
        in_specs=[row(Q_NSA), row(Q_SB), row(2 * D_MODEL), wspec(Q_NSA), wspec(Q_SB), wspec(D_MODEL),
                  row(D_MODEL), mod_spec, vec, vec, mod_spec, mod_spec],
        out_specs=(row(D_MODEL), row(D_MODEL)),
        out_shape=(jax.ShapeDtypeStruct((rows, D_MODEL), F32), jax.ShapeDtypeStruct((rows, D_MODEL), BF16)),
        compiler_params=_cparams(("arbitrary",)),
        name="merge_out",
    )(oa, ob, mg, wa, wb, wo, x, gate, ng1, ng2, shift, scale)


def _ffn_kernel(h_ref, wg_ref, wu_ref, wd_ref, x1_ref, gate_ref, ng_ref, o_ref, acc_ref):
    k = pl.program_id(1)

    @pl.when(k == 0)
    def _():
        acc_ref[...] = jnp.zeros_like(acc_ref)

    h = h_ref[...]
    g = _dot(h, wg_ref[...])
    u = _dot(h, wu_ref[...])
    act = (g * jax.nn.sigmoid(g)) * u
    acc_ref[...] += _dot(act.astype(BF16), wd_ref[...])

    @pl.when(k == pl.num_programs(1) - 1)
    def _():
        o_ref[...] = x1_ref[...] + gate_ref[...] * (_rms(acc_ref[...]) * ng_ref[...])


def _ffn_call(h2, w_up, w_down, x1, gate, ng, tm):
    rows = h2.shape[0]
    d_ff = w_down.shape[0]
    tk = 256
    nk = d_ff // tk
    assert d_ff % tk == 0
    per_row = gate.shape[0] != 1
    mod_spec = (pl.BlockSpec((tm, D_MODEL), lambda i, k: (i, 0)) if per_row
                else pl.BlockSpec((1, D_MODEL), lambda i, k: (0, 0)))
    row = pl.BlockSpec((tm, D_MODEL), lambda i, k: (i, 0))
    return pl.pallas_call(
        _ffn_kernel,
        grid=(rows // tm, nk),
        in_specs=[row,
                  pl.BlockSpec((D_MODEL, tk), lambda i, k: (0, k)),
                  pl.BlockSpec((D_MODEL, tk), lambda i, k: (0, nk + k)),
                  pl.BlockSpec((tk, D_MODEL), lambda i, k: (k, 0)),
                  row, mod_spec,
                  pl.BlockSpec((1, D_MODEL), lambda i, k: (0, 0))],
        out_specs=row,
        out_shape=jax.ShapeDtypeStruct((rows, D_MODEL), F32),
        scratch_shapes=[pltpu.VMEM((tm, D_MODEL), F32)],
        compiler_params=_cparams(("arbitrary", "arbitrary")),
        name="ffn",
    )(h2, w_up, w_up, w_down, x1, gate, ng)


def _nsa_s1_kernel(q_ref, kc_ref, vct_ref, oc_ref, idx_ref, *, nt, past_len, n_cmp, n_sel, topk):
    n_cmp_pad = kc_ref.shape[1]
    n_sel_pad = ((n_sel + LANES - 1) // LANES) * LANES
    rows = GROUP_NSA * nt
    q_pos = past_len + lax.broadcasted_iota(I32, (rows, 1), 0) % nt
    n = lax.broadcasted_iota(I32, (rows, n_cmp_pad), 1)
    mask = (n * CMP_STRIDE + CMP_LEN - 1 <= q_pos) & (n < n_cmp)
    probs = []
    for g in range(N_KV_NSA):
        q = q_ref[g * rows:(g + 1) * rows, :]
        p = _masked_softmax(_dot_nt(q, kc_ref[g]), mask)
        oc_ref[g * rows:(g + 1) * rows, :] = _dot_nt(p.astype(BF16), vct_ref[g])
        probs.append(p)
    p_all = jnp.concatenate(probs, axis=0)
    ro = lax.broadcasted_iota(I32, (N_KV_NSA * nt, N_KV_NSA * rows), 0)
    ri = lax.broadcasted_iota(I32, (N_KV_NSA * nt, N_KV_NSA * rows), 1)
    pick = jnp.where((ro // nt == ri // rows) & (ro % nt == ri % nt), 1.0, 0.0).astype(BF16)
    p1, p2, p3 = _split3(p_all)
    p_sum = _dot(pick, p1) + _dot(pick, p2) + _dot(pick, p3)
    imp = _importance(p_sum, n_sel_pad, axis=1)
    t_pos = past_len + lax.broadcasted_iota(I32, (N_KV_NSA * nt, 1), 0) % nt
    score = _block_scores(imp, t_pos, n_sel, axis=1)
    lane = lax.broadcasted_iota(I32, (N_KV_NSA * nt, LANES), 1)
    out = jnp.full((N_KV_NSA * nt, LANES), -1.0, F32)
    for r, (val, idx) in enumerate(_topk_rounds(score, topk, axis=1)):
        out = jnp.where(lane == r, jnp.where(val > 0.5 * NEG_INF, idx, -1.0), out)
    idx_ref[...] = out.astype(I32)


def _nsa_s1_call(q1, kcmp, vcmpt, nt, past_len, n_cmp, n_sel):
    nb = q1.shape[0]
    n_half = kcmp.shape[2]
    kern = functools.partial(_nsa_s1_kernel, nt=nt, past_len=past_len, n_cmp=n_cmp, n_sel=n_sel,
                             topk=min(TOP_N, n_sel))
    rows = N_HEADS_NSA * nt
    per_b = lambda *shape: pl.BlockSpec((None,) + shape, lambda b: (b,) + (0,) * len(shape))
    return pl.pallas_call(
        kern,
        grid=(nb,),
        in_specs=[per_b(rows, HEAD_DIM), per_b(N_KV_NSA, n_half, HEAD_DIM), per_b(N_KV_NSA, HEAD_DIM, n_half)],
        out_specs=(per_b(rows, HEAD_DIM), per_b(N_KV_NSA * nt, LANES)),
        out_shape=(jax.ShapeDtypeStruct((nb, rows, HEAD_DIM), F32),
                   jax.ShapeDtypeStruct((nb, N_KV_NSA * nt, LANES), I32)),
        compiler_params=_cparams(("arbitrary",)),
        name="nsa_sample_topk",
    )(q1, kcmp, vcmpt)


def _nsa_s2_kernel(idx_s, pt_s, q_ref, idxv_ref, oc_ref, gate_ref, newkv_ref, wkv_ref, wnew_ref, cache_hbm,
                   o_ref, kbuf, vbuf, sem, os_scr, *, nt, n_pages, n_past_blocks, topk):
    b = pl.program_id(0)
    ngt = N_KV_NSA * nt
    bpp = PAGE // SEL_BLOCK

    def slot_copies(j, gt, s):
        page = pt_s[b * n_pages + j // bpp]
        dst = lambda buf: buf.at[gt, :, pl.ds(pl.multiple_of(s * PAGE, PAGE), PAGE)]
        return (pltpu.make_async_copy(cache_hbm.at[0, page, 2, gt // nt], dst(kbuf), sem.at[0]),
                pltpu.make_async_copy(cache_hbm.at[0, page, 3, gt // nt], dst(vbuf), sem.at[0]))

    def issue(n, _):
        gt, s = n // topk, n % topk
        j = idx_s[(b * ngt + gt) * topk + s]
        valid = (j >= 0) & (j < n_past_blocks)

        @pl.when(valid)
        def _():
            for cpy in slot_copies(j, gt, s):
                cpy.start()

        @pl.when(jnp.logical_not(valid))
        def _():
            zeros = jnp.zeros((HEAD_DIM, PAGE), F32)
            kbuf[gt, :, pl.ds(pl.multiple_of(s * PAGE, PAGE), PAGE)] = zeros
            vbuf[gt, :, pl.ds(pl.multiple_of(s * PAGE, PAGE), PAGE)] = zeros
        return 0

    def drain(n, _):
        gt, s = n // topk, n % topk
        j = idx_s[(b * ngt + gt) * topk + s]

        @pl.when((j >= 0) & (j < n_past_blocks))
        def _():
            for cpy in slot_copies(j, gt, s):
                cpy.wait()
        return 0

    lax.fori_loop(0, ngt * topk, issue, 0)

    rows = GROUP_NSA * nt
    t_row = lax.broadcasted_iota(I32, (rows, 1), 0) // GROUP_NSA
    nw = wkv_ref.shape[1]
    nn = newkv_ref.shape[1]
    new_ok = lax.broadcasted_iota(I32, (rows, nn), 1) <= t_row
    win_ok = lax.broadcasted_iota(I32, (rows, nw), 1) > t_row
    o_w = []
    for g in range(N_KV_NSA):
        q = q_ref[g * rows:(g + 1) * rows, :]
        s_old = jnp.where(win_ok, _dot_nt(q, wkv_ref[g]), NEG_INF)
        s_new = jnp.where(new_ok, _dot_nt(q, wnew_ref[g]), NEG_INF)
        m = jnp.maximum(jnp.max(s_old, axis=-1, keepdims=True), jnp.max(s_new, axis=-1, keepdims=True))
        e_old = jnp.where(win_ok, jnp.exp(s_old - m), 0.0)
        e_new = jnp.where(new_ok, jnp.exp(s_new - m), 0.0)
        l = jnp.sum(e_old, axis=-1, keepdims=True) + jnp.sum(e_new, axis=-1, keepdims=True)
        pv = _dot(e_old.astype(BF16), wkv_ref[g]) + _dot(e_new.astype(BF16), wnew_ref[g])
        o_w.append(pv[:, HEAD_DIM:] / jnp.maximum(l, TINY))
    o_w = jnp.concatenate(o_w, axis=0)

    lax.fori_loop(0, ngt * topk, drain, 0)

    idxv = idxv_ref[...]
    slot_valid = (idxv >= 0) & (idxv < n_past_blocks)
    odd = (idxv & (bpp - 1)) == 1
    es = lax.broadcasted_iota(I32, (LANES, topk * PAGE), 0)
    ek = lax.broadcasted_iota(I32, (LANES, topk * PAGE), 1)
    in_slot = es == ek // PAGE
    key_odd = (ek // SEL_BLOCK) % bpp == 1
    pick = lambda cond: jnp.where(cond, 1.0, 0.0).astype(BF16)
    key_valid = (_dot(pick(slot_valid & jnp.logical_not(odd)), pick(in_slot & jnp.logical_not(key_odd)))
                 + _dot(pick(slot_valid & odd), pick(in_slot & key_odd))) > 0.5
    for gt in range(ngt):
        g, t = gt // nt, gt % nt
        q = q_ref[gt * GROUP_NSA:(gt + 1) * GROUP_NSA, :]
        kvn = newkv_ref[g]
        ok_old = key_valid[gt:gt + 1, :]
        ok_new = lax.broadcasted_iota(I32, (GROUP_NSA, nn), 1) <= t
        s_old = jnp.where(ok_old, _dot(q[:, :HEAD_DIM], kbuf[gt].astype(BF16)), NEG_INF)
        s_new = jnp.where(ok_new, _dot_nt(q, kvn), NEG_INF)
        m = jnp.maximum(jnp.max(s_old, axis=-1, keepdims=True), jnp.max(s_new, axis=-1, keepdims=True))
        e_old = jnp.where(ok_old, jnp.exp(s_old - m), 0.0)
        e_new = jnp.where(ok_new, jnp.exp(s_new - m), 0.0)
        l = jnp.sum(e_old, axis=-1, keepdims=True) + jnp.sum(e_new, axis=-1, keepdims=True)
        pv = _dot_nt(e_old.astype(BF16), vbuf[gt].astype(BF16)) + _dot(e_new.astype(BF16), kvn)[:, HEAD_DIM:]
        os_scr[gt * GROUP_NSA:(gt + 1) * GROUP_NSA, :] = pv / jnp.maximum(l, TINY)

    gates = gate_ref[...]
    o_ref[...] = gates[:, 0:1] * oc_ref[...] + gates[:, 1:2] * os_scr[...] + gates[:, 2:3] * o_w


def _nsa_s2_call(idx, page_table, q2, oc, gates, newkv, wkv, wnew, cache, nt, past_len):
    nb, n_pages = page_table.shape
    topk = TOP_N
    ngt = N_KV_NSA * nt
    rows = N_HEADS_NSA * nt
    kern = functools.partial(_nsa_s2_kernel, nt=nt, n_pages=n_pages, n_past_blocks=past_len // SEL_BLOCK, topk=topk)
    per_b = lambda *shape: pl.BlockSpec((None,) + shape, lambda b, i_s, p_s: (b,) + (0,) * len(shape))
    grid_spec = pltpu.PrefetchScalarGridSpec(
        num_scalar_prefetch=2,
        grid=(nb,),
        in_specs=[per_b(rows, LANES), per_b(ngt, LANES), per_b(rows, HEAD_DIM), per_b(rows, LANES),
                  per_b(N_KV_NSA, newkv.shape[2], LANES), per_b(N_KV_NSA, wkv.shape[2], LANES),
                  per_b(N_KV_NSA, wnew.shape[2], LANES), pl.BlockSpec(memory_space=pl.ANY)],
        out_specs=per_b(rows, HEAD_DIM),
        scratch_shapes=[pltpu.VMEM((ngt, HEAD_DIM, topk * PAGE), F32),
                        pltpu.VMEM((ngt, HEAD_DIM, topk * PAGE), F32),
                        pltpu.SemaphoreType.DMA((1,)),
                        pltpu.VMEM((rows, HEAD_DIM), F32)],
    )
    return pl.pallas_call(
        kern, grid_spec=grid_spec,
        out_shape=jax.ShapeDtypeStruct((nb, rows, HEAD_DIM), F32),
        compiler_params=_cparams(("arbitrary",)),
        name="nsa_sample_gather",
    )(idx[:, :, :topk].reshape(-1), page_table.reshape(-1), q2, idx, oc, gates, newkv, wkv, wnew, cache)


def _sb_sample_kernel(pt_s, qbd_ref, newk_ref, newv_ref, cache_hbm, o_ref, buf, sem, *, nt, n_pages):
    b = pl.program_id(0)
    nh = N_HEADS_SB
    rows = nt * nh
    qbd = qbd_ref[...]
    own_head = (lax.broadcasted_iota(I32, (rows, Q_SB), 0) % nh
                == lax.broadcasted_iota(I32, (rows, Q_SB), 1) // HEAD_DIM)

    def step(kt, vt, mask, carry, acc):
        a, carry = _sb_weights(_dot(qbd, kt), mask, carry)
        return carry, acc + jnp.where(own_head, _dot_nt(a.astype(BF16), vt), 0.0)

    def copy(pg, slot):
        return pltpu.make_async_copy(cache_hbm.at[0, pt_s[b * n_pages + pg]], buf.at[slot], sem.at[slot])

    copy(n_pages - 1, 0).start()
    t_row = lax.broadcasted_iota(I32, (rows, PAGE), 0) // nh
    new_mask = lax.broadcasted_iota(I32, (rows, PAGE), 1) < t_row
    carry, acc = step(newk_ref[...], newv_ref[...], new_mask,
                      jnp.zeros((rows, 1), F32), jnp.zeros((rows, Q_SB), F32))

    def cond(st):
        return (st[0] >= 0) & (st[4] > SB_EXIT)

    def body(st):
        pg, slot, carry, acc, _ = st
        copy(pg, slot).wait()

        @pl.when(pg > 0)
        def _():
            copy(pg - 1, 1 - slot).start()

        kt = buf[slot, 0].reshape(Q_SB, PAGE).astype(BF16)
        vt = buf[slot, 1].reshape(Q_SB, PAGE).astype(BF16)
        carry, acc = step(kt, vt, None, carry, acc)
        return pg - 1, 1 - slot, carry, acc, jnp.max(carry)

    pg, slot, _, acc, _ = lax.while_loop(cond, body, (jnp.int32(n_pages - 1), jnp.int32(0), carry, acc, jnp.max(carry)))

    @pl.when(pg >= 0)
    def _():
        copy(pg, slot).wait()

    o_ref[...] = jnp.sum(acc.reshape(nt, nh, Q_SB), axis=1)


def _sb_sample_call(page_table, qbd, newk, newv, cache, nt):
    nb, n_pages = page_table.shape
    nh = N_HEADS_SB
    kern = functools.partial(_sb_sample_kernel, nt=nt, n_pages=n_pages)
    per_b = lambda *shape: pl.BlockSpec((None,) + shape, lambda b, p_s: (b,) + (0,) * len(shape))
    grid_spec = pltpu.PrefetchScalarGridSpec(
        num_scalar_prefetch=1,
        grid=(nb,),
        in_specs=[per_b(nt * nh, Q_SB), per_b(Q_SB, PAGE), per_b(Q_SB, PAGE), pl.BlockSpec(memory_space=pl.ANY)],
        out_specs=per_b(nt, Q_SB),
        scratch_shapes=[pltpu.VMEM((2, 2, nh, HEAD_DIM, PAGE), F32), pltpu.SemaphoreType.DMA((2,))],
    )
    return pl.pallas_call(
        kern, grid_spec=grid_spec,
        out_shape=jax.ShapeDtypeStruct((nb, nt, Q_SB), F32),
        compiler_params=_cparams(("arbitrary",)),
        name="sb_sample",
    )(page_table.reshape(-1), qbd, newk, newv, cache)


def _rope_tables(pos):
    half = ROPE_DIM // 2
    inv_freq = ROPE_THETA ** (-jnp.arange(half, dtype=F32) / half)
    ang = pos.astype(F32)[:, None] * inv_freq[None, :]
    cos, sin = jnp.cos(ang), jnp.sin(ang)
    one = jnp.ones((pos.shape[0], HEAD_DIM - ROPE_DIM), F32)
    c64 = jnp.concatenate([cos, cos, one], axis=1)
    s64 = jnp.concatenate([-sin, sin, jnp.zeros_like(one)], axis=1)
    return jnp.tile(c64, (1, LANES // HEAD_DIM)), jnp.tile(s64, (1, LANES // HEAD_DIM))


def _pad_w_in(w_in):
    c_ga = 512 + 768
    n_ga = 3 * N_HEADS_NSA
    ga = w_in[:, c_ga:c_ga + n_ga].reshape(D_MODEL, N_KV_NSA, n_ga // N_KV_NSA)
    ga = jnp.pad(ga, ((0, 0), (0, 0), (0, LANES - n_ga // N_KV_NSA))).reshape(D_MODEL, N_KV_NSA * LANES)
    return jnp.concatenate([w_in[:, :c_ga], ga, w_in[:, c_ga + n_ga:]], axis=1).astype(BF16)


def _pad_axis(x, axis, size):
    pad = [(0, 0)] * x.ndim
    pad[axis] = (0, size - x.shape[axis])
    return jnp.pad(x, pad)


def kernel(x_prompt, x_sample, cache_nsa_kv, cache_sb_kv, state_win_kv, page_table, c_prompt, c_sample, ada_w, ada_b, norm_g, w_in, cmp_pe, cmp_w1, cmp_b1, cmp_w2, cmp_b2, w_o_nsa, w_o_sb, w_out, ffn_w_up, ffn_w_down):
    assert ada_w.shape[0] == 1 and x_prompt.shape[0] == 1
    t = x_prompt.shape[1]
    nb, nt = x_sample.shape[:2]
    n_pages = page_table.shape[1]
    past_len = n_pages * PAGE
    assert nt <= 8 and state_win_kv.shape[2] == WINDOW and t >= WINDOW

    w = _pad_w_in(w_in[0])
    cw = _compress_weights(cmp_pe[0], cmp_w1[0], cmp_b1[0], cmp_w2[0], cmp_b2[0])
    wa = w_o_nsa[0].astype(BF16)
    wb = w_o_sb[0].astype(BF16)
    wo = w_out[0].astype(BF16)
    wup = ffn_w_up[0].astype(BF16)
    wdn = ffn_w_down[0].astype(BF16)
    ng = norm_g[0]

    c_all = _pad_axis(jnp.concatenate([c_prompt, c_sample], axis=0), 0, -(-(1 + nb) // SUBLANES) * SUBLANES)
    mod = _mod_call(c_all, ada_w[0], ada_b[0][None])
    modp = [mod[0:1, k * D_MODEL:(k + 1) * D_MODEL] for k in range(6)]
    mods = [jnp.repeat(mod[1:1 + nb, k * D_MODEL:(k + 1) * D_MODEL], nt, axis=0) for k in range(6)]

    xp = x_prompt[0]
    cos, sin = _rope_tables(jnp.arange(t))
    (_, qat, nsa_rows, win_rows, kslc, vslct, kwin, vwint, _, gatest, qb, sb_rows, ksb, vsb, mg) = _proj_call(
        xp, modp[0], modp[1], ng[0:1], w, cos, sin, tm=256)
    ident = jnp.arange(t // PAGE, dtype=I32).reshape(1, -1)
    kcmp, vcmpt = _compress_call(ident, nsa_rows.reshape(t // PAGE, PAGE, 512), *cw)
    oa = _nsa_prompt_call(qat, kcmp[0], vcmpt[0], kslc, vslct, kwin, vwint, gatest, n_cmp=t // CMP_STRIDE - 1)
    ob = _sb_prompt_call(qb, ksb, vsb)
    x1, h2 = _merge_call(oa, ob, mg, wa, wb, wo, xp, modp[2], ng[1:2], ng[2:3], modp[3], modp[4], tm=256)
    y_prompt = _ffn_call(h2, wup, wdn, x1, modp[5], ng[3:4], tm=512)

    rs = nb * nt
    xs = x_sample.reshape(rs, D_MODEL)
    cos_s, sin_s = _rope_tables(past_len + jnp.tile(jnp.arange(nt), nb))
    (qa_s, _, nsa_new, win_new, _, _, _, _, gates_s, _, qb_s, sb_new, _, _, mg_s) = _proj_call(
        xs, mods[0], mods[1], ng[0:1], w, cos_s, sin_s, tm=rs)

    nsa_cache = cache_nsa_kv.transpose(0, 1, 3, 4, 5, 2)
    sb_cache = cache_sb_kv.transpose(0, 1, 3, 4, 5, 2)
    kcmp_s, vcmpt_s = _compress_call(page_table, nsa_cache, *cw)
    n_cmp = past_len // CMP_STRIDE - 1
    n_sel = past_len // SEL_BLOCK + 1

    qa5 = qa_s.reshape(nb, nt, N_KV_NSA, GROUP_NSA, HEAD_DIM)
    q1 = qa5.transpose(0, 2, 3, 1, 4).reshape(nb, N_HEADS_NSA * nt, HEAD_DIM)
    oc, idx = _nsa_s1_call(q1, kcmp_s, vcmpt_s, nt, past_len, n_cmp, n_sel)
    q2 = qa5.transpose(0, 2, 1, 3, 4).reshape(nb, N_HEADS_NSA * nt, HEAD_DIM)
    q2 = _pad_axis(q2, 2, LANES)
    oc2 = oc.reshape(nb, N_KV_NSA, GROUP_NSA, nt, HEAD_DIM).transpose(0, 1, 3, 2, 4).reshape(nb, -1, HEAD_DIM)
    g2 = gates_s.reshape(nb, nt, N_KV_NSA, LANES)[..., :3 * GROUP_NSA].reshape(nb, nt, N_KV_NSA, GROUP_NSA, 3)
    g2 = _pad_axis(g2.transpose(0, 2, 1, 3, 4).reshape(nb, N_HEADS_NSA * nt, 3), 2, LANES)

    def new_rows_kv(rows_kv):
        kv = rows_kv.reshape(nb, nt, 2, N_KV_NSA, HEAD_DIM).transpose(0, 3, 1, 2, 4).reshape(nb, N_KV_NSA, nt, LANES)
        return _pad_axis(kv, 2, SUBLANES).astype(BF16)

    newkv = new_rows_kv(nsa_new[:, 256:512])
    wnew = new_rows_kv(win_new)
    wkv = state_win_kv[0].transpose(0, 3, 1, 2, 4).reshape(nb, N_KV_NSA, WINDOW, LANES).astype(BF16)
    oa_s = _nsa_s2_call(idx, page_table, q2, oc2, g2, newkv, wkv, wnew, nsa_cache, nt, past_len)
    oa_s = oa_s.reshape(nb, N_KV_NSA, nt, GROUP_NSA, HEAD_DIM).transpose(0, 2, 1, 3, 4).reshape(rs, Q_NSA).astype(BF16)

    qb4 = qb_s.reshape(nb, nt, N_HEADS_SB, 1, HEAD_DIM)
    eye = jnp.eye(N_HEADS_SB, dtype=BF16)[None, None, :, :, None]
    qbd = (qb4 * eye).reshape(nb, nt * N_HEADS_SB, Q_SB)
    new_sb = sb_new.reshape(nb, nt, 2, N_HEADS_SB, HEAD_DIM).transpose(2, 0, 3, 4, 1)
    new_sb = _pad_axis(new_sb, 4, PAGE).reshape(2, nb, Q_SB, PAGE).astype(BF16)
    ob_s = _sb_sample_call(page_table, qbd, new_sb[0], new_sb[1], sb_cache, nt)
    ob_s = ob_s.reshape(rs, Q_SB).astype(BF16)

    x1s, h2s = _merge_call(oa_s, ob_s, mg_s, wa, wb, wo, xs, mods[2], ng[1:2], ng[2:3], mods[3], mods[4], tm=rs)
    y_sample = _ffn_call(h2s, wup, wdn, x1s, mods[5], ng[3:4], tm=rs)

    win_new5 = win_new.reshape(nb, nt, 2, N_KV_NSA, HEAD_DIM)
    win_kv_sample = jnp.concatenate([state_win_kv[0], win_new5], axis=1)[:, nt:][:, -WINDOW:]
    return (y_prompt[None],
            y_sample.reshape(nb, nt, D_MODEL),
            nsa_rows.reshape(1, 1, t, 4, N_KV_NSA, HEAD_DIM),
            sb_rows.reshape(1, 1, t, 2, N_HEADS_SB, HEAD_DIM),
            win_rows[t - WINDOW:].reshape(1, 1, WINDOW, 2, N_KV_NSA, HEAD_DIM),
            nsa_new.reshape(1, nb, nt, 4, N_KV_NSA, HEAD_DIM),
            sb_new.reshape(1, nb, nt, 2, N_HEADS_SB, HEAD_DIM),
            win_kv_sample[None])
```

```python
import functools

import jax
import jax.numpy as jnp
from jax import lax
from jax.experimental import pallas as pl
from jax.experimental.pallas import tpu as pltpu

F32 = jnp.float32
BF16 = jnp.bfloat16
I32 = jnp.int32

D_MODEL = 1024
HEAD_DIM = 64
N_HEADS_NSA = 8
N_KV_NSA = 2
GROUP_NSA = N_HEADS_NSA // N_KV_NSA
N_HEADS_SB = 8
ROPE_DIM = HEAD_DIM // 4
ROPE_THETA = 500000.0
CMP_STRIDE = 16
CMP_LEN = 2 * CMP_STRIDE
CMP_HIDDEN = 4 * HEAD_DIM
SEL_BLOCK = 64
SEL_RATIO = SEL_BLOCK // CMP_STRIDE
TOP_N = 16
WINDOW = 512
PAGE = 128
RMS_EPS = 1e-6
NEG_INF = -1e30
BIG = 1e30
TINY = 1e-30
ATTN_SCALE = HEAD_DIM ** -0.5
LOG2_E = 1.4426950408889634

LANES = 128
SUBLANES = 8
VMEM_LIMIT = 56 * 1024 * 1024
SB_EXIT = -104.0

C_QA = 0
C_KVA = 512
C_GA = 1280
C_QB = 1536
C_KVB = 2048
C_MG = 3072
C_END = 5120

Q_NSA = N_HEADS_NSA * HEAD_DIM
Q_SB = N_HEADS_SB * HEAD_DIM
QG = GROUP_NSA * HEAD_DIM


def _cparams(sem):
    return pltpu.CompilerParams(dimension_semantics=sem, vmem_limit_bytes=VMEM_LIMIT)


def _rms(x):
    return x * lax.rsqrt(jnp.mean(x * x, axis=-1, keepdims=True) + RMS_EPS)


def _dot(a, b):
    return jnp.dot(a, b, preferred_element_type=F32)


def _dot_nt(a, b):
    return lax.dot_general(a, b, (((1,), (1,)), ((), ())), preferred_element_type=F32)


def _split3(x):
    p1 = x.astype(BF16)
    r1 = x - p1.astype(F32)
    p2 = r1.astype(BF16)
    p3 = (r1 - p2.astype(F32)).astype(BF16)
    return p1, p2, p3


def _softplus(z):
    return jnp.maximum(z, 0.0) + jnp.log(1.0 + jnp.exp(-jnp.abs(z)))


def _mod_kernel(c_ref, w_ref, b_ref, o_ref):
    c = c_ref[...]
    s = c * jax.nn.sigmoid(c)
    o_ref[...] = _dot(s.astype(BF16), w_ref[...].astype(BF16)) + b_ref[...]


def _mod_call(c, w, b):
    rows, n = c.shape[0], w.shape[1]
    tn = 1536
    return pl.pallas_call(
        _mod_kernel,
        grid=(n // tn,),
        in_specs=[pl.BlockSpec((rows, D_MODEL), lambda j: (0, 0)),
                  pl.BlockSpec((D_MODEL, tn), lambda j: (0, j)),
                  pl.BlockSpec((1, tn), lambda j: (0, j))],
        out_specs=pl.BlockSpec((rows, tn), lambda j: (0, j)),
        out_shape=jax.ShapeDtypeStruct((rows, n), F32),
        compiler_params=_cparams(("arbitrary",)),
        name="adaln_mod",
    )(c, w, b)


def _rope(y, cos, sin):
    w = y.shape[1]
    reps = w // LANES
    if reps > 1:
        cos = jnp.concatenate([cos] * reps, axis=1)
        sin = jnp.concatenate([sin] * reps, axis=1)
    lane = lax.broadcasted_iota(I32, y.shape, 1) & (HEAD_DIM - 1)
    half = ROPE_DIM // 2
    partner = jnp.where(lane < half, pltpu.roll(y, w - half, 1), pltpu.roll(y, half, 1))
    return y * cos + partner * sin


def _proj_kernel(x_ref, shift_ref, scale_ref, ng_ref, w_ref, cos_ref, sin_ref,
                 qa_ref, qat_ref, nsa_ref, win_ref, kslc_ref, vslct_ref, kwin_ref, vwint_ref, gate_ref, gatet_ref,
                 qb_ref, sb_ref, ksb_ref, vsb_ref, mg_ref):
    x = x_ref[...]
    h = _rms(x) * ng_ref[...]
    h = h * (1.0 + scale_ref[...]) + shift_ref[...]
    hb = h.astype(BF16)
    cos = cos_ref[...]
    sin = sin_ref[...]

    def mm(c0, c1):
        return _dot(hb, w_ref[:, c0:c1])

    def heads(dst, y, n):
        for i in range(n):
            dst[i] = y[:, i * HEAD_DIM:(i + 1) * HEAD_DIM].astype(dst.dtype)

    def heads_t(dst, y, n):
        yt = y.T
        for i in range(n):
            dst[i] = yt[i * HEAD_DIM:(i + 1) * HEAD_DIM, :].astype(dst.dtype)

    qa = _rope(mm(C_QA, C_QA + Q_NSA), cos, sin) * (ATTN_SCALE * LOG2_E)
    qa_ref[...] = qa.astype(BF16)
    heads_t(qat_ref, qa, N_HEADS_NSA)

    kc = _rope(mm(C_KVA, C_KVA + 128), cos, sin)
    vc = mm(C_KVA + 128, C_KVA + 256)
    ks = _rope(mm(C_KVA + 256, C_KVA + 384), cos, sin)
    vs = mm(C_KVA + 384, C_KVA + 512)
    kw = _rope(mm(C_KVA + 512, C_KVA + 640), cos, sin)
    vw = mm(C_KVA + 640, C_KVA + 768)
    nsa_ref[:, 0:128] = kc
    nsa_ref[:, 128:256] = vc
    nsa_ref[:, 256:384] = ks
    nsa_ref[:, 384:512] = vs
    win_ref[:, 0:128] = kw
    win_ref[:, 128:256] = vw
    heads(kslc_ref, ks, N_KV_NSA)
    heads_t(vslct_ref, vs, N_KV_NSA)
    heads(kwin_ref, kw, N_KV_NSA)
    heads_t(vwint_ref, vw, N_KV_NSA)

    gates = jax.nn.sigmoid(mm(C_GA, C_GA + 2 * LANES))
    gate_ref[...] = gates
    gatet_ref[...] = gates.T

    qb_ref[...] = (mm(C_QB, C_QB + Q_SB) * ATTN_SCALE).astype(BF16)

    kb = mm(C_KVB, C_KVB + Q_SB)
    vb = mm(C_KVB + Q_SB, C_KVB + 2 * Q_SB)
    sb_ref[:, 0:Q_SB] = kb
    sb_ref[:, Q_SB:2 * Q_SB] = vb
    ksb_ref[...] = kb.astype(BF16)
    vsb_ref[...] = vb.astype(BF16)

    for j in range(2):
        mg_ref[:, j * D_MODEL:(j + 1) * D_MODEL] = jax.nn.sigmoid(
            mm(C_MG + j * D_MODEL, C_MG + (j + 1) * D_MODEL)).astype(BF16)


def _proj_call(x, shift, scale, ng, w, cos, sin, tm):
    rows = x.shape[0]
    per_row = shift.shape[0] != 1
    mod_spec = (pl.BlockSpec((tm, D_MODEL), lambda i: (i, 0)) if per_row
                else pl.BlockSpec((1, D_MODEL), lambda i: (0, 0)))
    row = lambda n: pl.BlockSpec((tm, n), lambda i: (i, 0))
    hm = lambda n: pl.BlockSpec((n, tm, HEAD_DIM), lambda i: (0, i, 0))
    hmt = lambda n: pl.BlockSpec((n, HEAD_DIM, tm), lambda i: (0, 0, i))
    sds = jax.ShapeDtypeStruct
    out_shape = (
        sds((rows, Q_NSA), BF16),
        sds((N_HEADS_NSA, HEAD_DIM, rows), BF16),
        sds((rows, 512), F32),
        sds((rows, 256), F32),
        sds((N_KV_NSA, rows, HEAD_DIM), BF16),
        sds((N_KV_NSA, HEAD_DIM, rows), BF16),
        sds((N_KV_NSA, rows, HEAD_DIM), BF16),
        sds((N_KV_NSA, HEAD_DIM, rows), BF16),
        sds((rows, 2 * LANES), F32),
        sds((2 * LANES, rows), F32),
        sds((rows, Q_SB), BF16),
        sds((rows, 2 * Q_SB), F32),
        sds((rows, Q_SB), BF16),
        sds((rows, Q_SB), BF16),
        sds((rows, 2 * D_MODEL), BF16),
    )
    out_specs = (row(Q_NSA), hmt(8), row(512), row(256), hm(2), hmt(2), hm(2), hmt(2), row(2 * LANES),
                 pl.BlockSpec((2 * LANES, tm), lambda i: (0, i)),
                 row(Q_SB), row(2 * Q_SB), row(Q_SB), row(Q_SB), row(2 * D_MODEL))
    return pl.pallas_call(
        _proj_kernel,
        grid=(rows // tm,),
        in_specs=[row(D_MODEL), mod_spec, mod_spec,
                  pl.BlockSpec((1, D_MODEL), lambda i: (0, 0)),
                  pl.BlockSpec((D_MODEL, C_END), lambda i: (0, 0)),
                  row(LANES), row(LANES)],
        out_specs=out_specs,
        out_shape=out_shape,
        compiler_params=_cparams(("arbitrary",)),
        name="proj_in",
    )(x, shift, scale, ng, w, cos, sin)


def _compress_kernel(pt_ref, pages_hbm, w1_ref, bias_ref, w2_ref, b2_ref, kc_ref, vct_ref,
                     buf, stage, sem, hlo, hhi, *, n_pages, cp, token_minor):
    b = pl.program_id(0)
    nch = n_pages // cp
    nh = cp * (PAGE // CMP_STRIDE)

    def copies(page, slot, k):
        if token_minor:
            return [pltpu.make_async_copy(pages_hbm.at[0, page, pl.ds(0, 2)], stage.at[slot, k], sem.at[slot])]
        return [pltpu.make_async_copy(pages_hbm.at[page, :, pl.ds(c * LANES, LANES)],
                                      buf.at[slot, c, pl.ds(k * PAGE, PAGE)], sem.at[slot]) for c in range(2)]

    def start(ch, slot):
        for k in range(cp):
            for cpy in copies(pt_ref[b * n_pages + ch * cp + k], slot, k):
                cpy.start()

    def wait(slot):
        for k in range(cp):
            for cpy in copies(0, slot, k):
                cpy.wait()

    def to_token_major(slot):
        for k in range(cp):
            for c in range(2):
                m = stage[slot, k, c].reshape(N_KV_NSA * HEAD_DIM, PAGE)
                buf[slot, c, pl.ds(k * PAGE, PAGE), :] = m.T

    low = lax.broadcasted_iota(I32, (nh, LANES), 1) < HEAD_DIM

    def regroup(a, b_):
        return (jnp.where(low, a, pltpu.roll(b_, HEAD_DIM, 1)), jnp.where(low, pltpu.roll(a, HEAD_DIM, 1), b_))

    def first_layer(ch, slot):
        row0 = ch * nh if isinstance(ch, int) else pl.multiple_of(ch * nh, nh)
        for c in range(2):
            accs = [jnp.zeros((nh, 2 * CMP_HIDDEN), F32) for _ in range(N_KV_NSA)]
            for pq in range(CMP_STRIDE // 4):
                xp = [buf[slot, c, pl.ds(4 * pq + j, nh, stride=CMP_STRIDE), :] for j in range(4)]
                ab, cd = regroup(xp[0], xp[1]), regroup(xp[2], xp[3])
                for g in range(N_KV_NSA):
                    x4 = jnp.concatenate([ab[g], cd[g]], axis=1).astype(BF16)
                    accs[g] = accs[g] + _dot(x4, w1_ref[c, pq])
            for g in range(N_KV_NSA):
                cg = c * N_KV_NSA + g
                hlo[pl.ds(row0, nh), cg * CMP_HIDDEN:(cg + 1) * CMP_HIDDEN] = accs[g][:, :CMP_HIDDEN]
                hhi[pl.ds(row0, nh), cg * CMP_HIDDEN:(cg + 1) * CMP_HIDDEN] = accs[g][:, CMP_HIDDEN:]

    start(0, 0)
    if token_minor:
        wait(0)
        if nch > 1:
            start(1, 1)
        to_token_major(0)

        def pipelined(ch, _):
            slot = ch % 2
            wait(1 - slot)

            @pl.when(ch + 2 < nch)
            def _():
                start(ch + 2, slot)

            to_token_major(1 - slot)
            first_layer(ch, slot)
            return 0

        lax.fori_loop(0, nch - 1, pipelined, 0)
        first_layer(nch - 1, (nch - 1) % 2)
    else:
        for ch in range(nch):
            slot = ch % 2
            if ch + 1 < nch:
                start(ch + 1, 1 - slot)
            wait(slot)
            first_layer(ch, slot)
    n_half = nch * nh
    hhi[pl.ds(n_half, SUBLANES), :] = jnp.zeros((SUBLANES, 4 * CMP_HIDDEN), F32)
    pre = hlo[...] + hhi[pl.ds(1, n_half), :] + bias_ref[...]
    act = jax.nn.gelu(pre).astype(BF16)
    out = _dot(act, w2_ref[...]) + b2_ref[...]
    out_t = out.T
    for g in range(N_KV_NSA):
        kc_ref[g] = out[:, g * HEAD_DIM:(g + 1) * HEAD_DIM].astype(BF16)
        vct_ref[g] = out_t[(2 + g) * HEAD_DIM:(3 + g) * HEAD_DIM, :].astype(BF16)


def _compress_call(page_table, pages, w1, bias, w2, b2):
    nb, n_pages = page_table.shape
    cp = min(32, n_pages)
    assert n_pages % cp == 0
    n_half = n_pages * (PAGE // CMP_STRIDE)
    token_minor = pages.ndim == 6
    stage_shape = (2, cp, 2, N_KV_NSA, HEAD_DIM, PAGE) if token_minor else (SUBLANES, LANES)
    kern = functools.partial(_compress_kernel, n_pages=n_pages, cp=cp, token_minor=token_minor)
    const = lambda shape: pl.BlockSpec(shape, lambda b, pt: (0,) * len(shape))
    grid_spec = pltpu.PrefetchScalarGridSpec(
        num_scalar_prefetch=1,
        grid=(nb,),
        in_specs=[pl.BlockSpec(memory_space=pl.ANY),
                  const((2, CMP_STRIDE // 4, 4 * HEAD_DIM, 2 * CMP_HIDDEN)),
                  const((1, 4 * CMP_HIDDEN)),
                  const((4 * CMP_HIDDEN, 4 * HEAD_DIM)),
                  const((1, 4 * HEAD_DIM))],
        out_specs=(pl.BlockSpec((None, N_KV_NSA, n_half, HEAD_DIM), lambda b, pt: (b, 0, 0, 0)),
                   pl.BlockSpec((None, N_KV_NSA, HEAD_DIM, n_half), lambda b, pt: (b, 0, 0, 0))),
        scratch_shapes=[pltpu.VMEM((2, 2, cp * PAGE, LANES), F32),
                        pltpu.VMEM(stage_shape, F32),
                        pltpu.SemaphoreType.DMA((2,)),
                        pltpu.VMEM((n_half, 4 * CMP_HIDDEN), F32),
                        pltpu.VMEM((n_half + SUBLANES, 4 * CMP_HIDDEN), F32)],
    )
    return pl.pallas_call(
        kern, grid_spec=grid_spec,
        out_shape=(jax.ShapeDtypeStruct((nb, N_KV_NSA, n_half, HEAD_DIM), BF16),
                   jax.ShapeDtypeStruct((nb, N_KV_NSA, HEAD_DIM, n_half), BF16)),
        compiler_params=_cparams(("arbitrary",)),
        name="compress_kv",
    )(page_table.reshape(-1), pages, w1, bias, w2, b2)


def _compress_weights(cmp_pe, cmp_w1, cmp_b1, cmp_w2, cmp_b2):
    w1r = cmp_w1.reshape(2, CMP_LEN, HEAD_DIM, CMP_HIDDEN)
    w1 = jnp.concatenate([w1r[:, :CMP_STRIDE], w1r[:, CMP_STRIDE:]], axis=-1).astype(BF16)
    w1 = w1.reshape(2, CMP_STRIDE // 4, 4 * HEAD_DIM, 2 * CMP_HIDDEN)
    bias = jnp.einsum('cpd,cpdh->ch', cmp_pe, w1r, precision=lax.Precision.HIGHEST) + cmp_b1
    bias = jnp.repeat(bias, N_KV_NSA, axis=0).reshape(1, 4 * CMP_HIDDEN)
    w2 = jnp.zeros((4, CMP_HIDDEN, 4, HEAD_DIM), F32)
    for cg in range(4):
        w2 = w2.at[cg, :, cg, :].set(cmp_w2[cg // 2])
    w2 = w2.reshape(4 * CMP_HIDDEN, 4 * HEAD_DIM).astype(BF16)
    b2 = jnp.repeat(cmp_b2, N_KV_NSA, axis=0).reshape(1, 4 * HEAD_DIM)
    return w1, bias, w2, b2


def _masked_softmax(s, mask, axis=-1):
    s = jnp.where(mask, s, NEG_INF)
    m = jnp.max(s, axis=axis, keepdims=True)
    e = jnp.where(mask, jnp.exp2(s - m), 0.0)
    return e / jnp.maximum(jnp.sum(e, axis=axis, keepdims=True), TINY)


def _tap_matrix(n_cmp_pad, n_sel_pad, axis):
    shape = (n_cmp_pad, n_sel_pad) if axis == 1 else (n_sel_pad, n_cmp_pad)
    n = lax.broadcasted_iota(I32, shape, 1 - axis)
    j = lax.broadcasted_iota(I32, shape, axis)
    front = CMP_LEN // CMP_STRIDE - 1
    return jnp.where((n >= SEL_RATIO * j - front) & (n <= SEL_RATIO * j + SEL_RATIO - 1), 1.0, 0.0).astype(BF16)


def _importance(p_sum, n_sel_pad, axis):
    taps = _tap_matrix(p_sum.shape[axis], n_sel_pad, axis)
    p1, p2, p3 = _split3(p_sum)
    if axis == 1:
        return _dot(p1, taps) + _dot(p2, taps) + _dot(p3, taps)
    return _dot(taps, p1) + _dot(taps, p2) + _dot(taps, p3)


def _topk_rounds(score, k, axis):
    n = score.shape[axis]
    blk = lax.broadcasted_iota(I32, score.shape, axis).astype(F32)
    outs = []
    work = score
    for _ in range(k):
        mx = jnp.max(work, axis=axis, keepdims=True)
        idx = jnp.min(jnp.where(work == mx, blk, float(n)), axis=axis, keepdims=True)
        outs.append((mx, idx))
        work = jnp.where(blk == idx, -jnp.inf, work)
    return outs


def _block_scores(imp, q_pos, n_sel, axis):
    blk = lax.broadcasted_iota(I32, imp.shape, axis)
    cur = q_pos // SEL_BLOCK
    forced = (blk == 0) | (blk == cur) | (blk == cur - 1)
    score = jnp.where(forced, BIG, jnp.where(blk <= cur, imp, NEG_INF))
    return jnp.where(blk < n_sel, score, -jnp.inf)


def _nsa_prompt_kernel(qt_ref, kc_ref, vct_ref, ks_ref, vst_ref, kw_ref, vwt_ref, gatet_ref, o_ref, bias_scr,
                       *, qb, kb, n_cmp, n_sel, topk):
    i = pl.program_id(1)
    n_cmp_pad = kc_ref.shape[0]
    nq = GROUP_NSA * qb
    qt = jnp.concatenate([qt_ref[r] for r in range(GROUP_NSA)], axis=1)
    q_pos1 = i * qb + lax.broadcasted_iota(I32, (1, qb), 1)
    q_pos = jnp.concatenate([q_pos1] * GROUP_NSA, axis=1)

    s = _dot(kc_ref[...], qt)
    n = lax.broadcasted_iota(I32, (n_cmp_pad, nq), 0)
    p_c = _masked_softmax(s, (n * CMP_STRIDE + CMP_LEN - 1 <= q_pos) & (n < n_cmp), axis=0)
    o_c = _dot(vct_ref[...], p_c.astype(BF16))

    p_sum = p_c[:, 0:qb]
    for r in range(1, GROUP_NSA):
        p_sum = p_sum + p_c[:, r * qb:(r + 1) * qb]
    imp = _importance(p_sum, n_sel, axis=0)
    score = _block_scores(imp, q_pos1, n_sel, axis=0)
    blk = lax.broadcasted_iota(I32, (n_sel, qb), 0).astype(F32)
    bias = jnp.full((n_sel, qb), NEG_INF, F32)
    for val, idx in _topk_rounds(score, topk, axis=0):
        bias = jnp.where((blk == idx) & (val > 0.5 * NEG_INF), 0.0, bias)
    bias_scr[...] = bias

    bpk = kb // SEL_BLOCK
    nkb = (i * qb + qb + kb - 1) // kb

    def sel_step(j, carry, causal):
        m, l, acc = carry
        start = pl.multiple_of(j * kb, kb)
        s = _dot(ks_ref[pl.ds(start, kb), :], qt)
        b8 = bias_scr[pl.ds(pl.multiple_of(j * bpk, bpk), bpk), :]
        b1 = jnp.broadcast_to(b8[:, None, :], (bpk, SEL_BLOCK, qb)).reshape(kb, qb)
        s = s + jnp.concatenate([b1] * GROUP_NSA, axis=1)
        if causal:
            k_pos = start + lax.broadcasted_iota(I32, (kb, nq), 0)
            s = jnp.where(k_pos <= q_pos, s, NEG_INF)
        m_new = jnp.maximum(m, jnp.max(s, axis=0, keepdims=True))
        e = jnp.exp2(s - m_new)
        alpha = jnp.exp2(m - m_new)
        l = alpha * l + jnp.sum(e, axis=0, keepdims=True)
        acc = alpha * acc + _dot(vst_ref[:, pl.ds(start, kb)], e.astype(BF16))
        return m_new, l, acc

    carry = (jnp.full((1, nq), NEG_INF, F32), jnp.zeros((1, nq), F32), jnp.zeros((HEAD_DIM, nq), F32))
    npair = (nkb - 1) // 2
    carry = lax.fori_loop(0, npair, lambda j, c: sel_step(2 * j + 1, sel_step(2 * j, c, False), False), carry)
    carry = lax.cond(nkb - 1 > 2 * npair, lambda c: sel_step(nkb - 2, c, False), lambda c: c, carry)
    _, l_s, acc_s = sel_step(nkb - 1, carry, True)
    o_s = acc_s / jnp.maximum(l_s, TINY)

    wlen = WINDOW + qb
    wstart = pl.multiple_of(jnp.maximum(i * qb - WINDOW, 0), qb)
    s = _dot(kw_ref[pl.ds(wstart, wlen), :], qt)
    dist = q_pos - (wstart + lax.broadcasted_iota(I32, (wlen, nq), 0))
    p_w = _masked_softmax(s, (dist >= 0) & (dist < WINDOW), axis=0)
    o_w = _dot(vwt_ref[:, pl.ds(wstart, wlen)], p_w.astype(BF16))

    gt = gatet_ref[...]
    gate = lambda br: jnp.concatenate([gt[3 * r + br:3 * r + br + 1, :] for r in range(GROUP_NSA)], axis=1)
    o = gate(0) * o_c + gate(1) * o_s + gate(2) * o_w
    o = jnp.concatenate([o[:, r * qb:(r + 1) * qb] for r in range(GROUP_NSA)], axis=0)
    o_ref[...] = o.T.astype(o_ref.dtype)


def _nsa_prompt_call(qat, kcmp, vcmpt, kslc, vslct, kwin, vwint, gatest, n_cmp):
    t = qat.shape[2]
    qb, kb = 128, 512
    assert t % kb == 0 and t >= WINDOW + qb
    n_sel = t // SEL_BLOCK
    n_cmp_pad = kcmp.shape[1]
    kern = functools.partial(_nsa_prompt_kernel, qb=qb, kb=kb, n_cmp=n_cmp, n_sel=n_sel, topk=min(TOP_N, n_sel))
    rows = lambda n: pl.BlockSpec((None, n, HEAD_DIM), lambda g, i: (g, 0, 0))
    cols = lambda n: pl.BlockSpec((None, HEAD_DIM, n), lambda g, i: (g, 0, 0))
    return pl.pallas_call(
        kern,
        grid=(N_KV_NSA, t // qb),
        in_specs=[pl.BlockSpec((GROUP_NSA, HEAD_DIM, qb), lambda g, i: (g, 0, i)),
                  rows(n_cmp_pad), cols(n_cmp_pad), rows(t), cols(t), rows(t), cols(t),
                  pl.BlockSpec((LANES, qb), lambda g, i: (g, i))],
        out_specs=pl.BlockSpec((qb, QG), lambda g, i: (i, g)),
        out_shape=jax.ShapeDtypeStruct((t, Q_NSA), BF16),
        scratch_shapes=[pltpu.VMEM((n_sel, qb), F32)],
        compiler_params=_cparams(("arbitrary", "arbitrary")),
        name="nsa_prompt",
    )(qat, kcmp, vcmpt, kslc, vslct, kwin, vwint, gatest)


def _sb_weights(z, mask, carry):
    kb = z.shape[1]
    l1 = -_softplus(z)
    if mask is not None:
        l1 = jnp.where(mask, l1, 0.0)
    sp = lax.broadcasted_iota(I32, (kb, kb), 0)
    sc = lax.broadcasted_iota(I32, (kb, kb), 1)
    later = jnp.where(sp > sc, 1.0, 0.0).astype(BF16)
    hi = l1.astype(BF16)
    lo = (l1 - hi.astype(F32)).astype(BF16)
    after = _dot(hi, later) + _dot(lo, later) + carry
    a = jnp.exp(z + l1 + after)
    if mask is not None:
        a = jnp.where(mask, a, 0.0)
    return a, carry + jnp.sum(l1, axis=-1, keepdims=True)


def _sb_prompt_kernel(q_ref, k_ref, v_ref, o_ref, *, qb, kb, nh):
    i = pl.program_id(1)
    width = nh * HEAD_DIM
    q = q_ref[...]
    own = [lax.broadcasted_iota(I32, (qb, width), 1) // HEAD_DIM == h for h in range(nh)]
    qh = [jnp.where(own[h], q, jnp.zeros_like(q)) for h in range(nh)]
    q_pos = i * qb + lax.broadcasted_iota(I32, (qb, 1), 0)

    def cond(st):
        return (st[0] > 0) & (st[3] > SB_EXIT)

    def body(st):
        hi, carries, acc, _ = st
        start = pl.multiple_of(jnp.maximum(hi - kb, 0), qb)
        k = k_ref[pl.ds(start, kb), :]
        v = v_ref[pl.ds(start, kb), :]
        mask = (start + lax.broadcasted_iota(I32, (qb, kb), 1)) < jnp.minimum(q_pos, hi)
        new_carries = []
        cmax = None
        for h in range(nh):
            a, c = _sb_weights(_dot_nt(qh[h], k), mask, carries[h])
            acc = acc + jnp.where(own[h], _dot(a.astype(BF16), v), 0.0)
            new_carries.append(c)
            cmax = c if cmax is None else jnp.maximum(cmax, c)
        return hi - kb, tuple(new_carries), acc, jnp.max(cmax)

    st = ((i + 1) * qb, tuple(jnp.zeros((qb, 1), F32) for _ in range(nh)), jnp.zeros((qb, width), F32),
          jnp.float32(0.0))
    acc = lax.while_loop(cond, body, st)[2]
    o_ref[...] = acc.astype(o_ref.dtype)


def _sb_prompt_call(qb_all, ksb, vsb):
    t = qb_all.shape[0]
    qb, kb, nh = 128, 384, 4
    assert t % qb == 0 and kb % qb == 0 and t >= kb
    width = nh * HEAD_DIM
    kern = functools.partial(_sb_prompt_kernel, qb=qb, kb=kb, nh=nh)
    whole = pl.BlockSpec((t, width), lambda h, i: (0, h))
    blk = pl.BlockSpec((qb, width), lambda h, i: (i, h))
    return pl.pallas_call(
        kern,
        grid=(N_HEADS_SB // nh, t // qb),
        in_specs=[blk, whole, whole],
        out_specs=blk,
        out_shape=jax.ShapeDtypeStruct((t, Q_SB), BF16),
        compiler_params=_cparams(("arbitrary", "arbitrary")),
        name="sb_prompt",
    )(qb_all, ksb, vsb)


def _merge_kernel(oa_ref, ob_ref, mg_ref, wa_ref, wb_ref, wo_ref, x_ref, gate_ref, ng1_ref, ng2_ref,
                  shift_ref, scale_ref, x1_ref, h2_ref):
    ya = _dot(oa_ref[...], wa_ref[...])
    yb = _dot(ob_ref[...], wb_ref[...])
    y = mg_ref[:, 0:D_MODEL].astype(F32) * ya + mg_ref[:, D_MODEL:2 * D_MODEL].astype(F32) * yb
    o = _dot(y.astype(BF16), wo_ref[...])
    x1 = x_ref[...] + gate_ref[...] * (_rms(o) * ng1_ref[...])
    x1_ref[...] = x1
    h2 = (_rms(x1) * ng2_ref[...]) * (1.0 + scale_ref[...]) + shift_ref[...]
    h2_ref[...] = h2.astype(BF16)


def _merge_call(oa, ob, mg, wa, wb, wo, x, gate, ng1, ng2, shift, scale, tm):
    rows = x.shape[0]
    per_row = gate.shape[0] != 1
    mod_spec = (pl.BlockSpec((tm, D_MODEL), lambda i: (i, 0)) if per_row
                else pl.BlockSpec((1, D_MODEL), lambda i: (0, 0)))
    vec = pl.BlockSpec((1, D_MODEL), lambda i: (0, 0))
    row = lambda n: pl.BlockSpec((tm, n), lambda i: (i, 0))
    wspec = lambda n: pl.BlockSpec((n, D_MODEL), lambda i: (0, 0))
    return pl.pallas_call(
        _merge_kernel,
        grid=(rows // tm,),
        in_specs=[row(Q_NSA), row(Q_SB), row(2 * D_MODEL), wspec(Q_NSA), wspec(Q_SB), wspec(D_MODEL),
                  row(D_MODEL), mod_spec, vec, vec, mod_spec, mod_spec],
        out_specs=(row(D_MODEL), row(D_MODEL)),
        out_shape=(jax.ShapeDtypeStruct((rows, D_MODEL), F32), jax.ShapeDtypeStruct((rows, D_MODEL), BF16)),
        compiler_params=_cparams(("arbitrary",)),
        name="merge_out",
    )(oa, ob, mg, wa, wb, wo, x, gate, ng1, ng2, shift, scale)


def _ffn_kernel(h_ref, wg_ref, wu_ref, wd_ref, x1_ref, gate_ref, ng_ref, o_ref, acc_ref):
    k = pl.program_id(1)

    @pl.when(k == 0)
    def _():
        acc_ref[...] = jnp.zeros_like(acc_ref)

    h = h_ref[...]
    g = _dot(h, wg_ref[...])
    u = _dot(h, wu_ref[...])
    act = (g * jax.nn.sigmoid(g)) * u
    acc_ref[...] += _dot(act.astype(BF16), wd_ref[...])

    @pl.when(k == pl.num_programs(1) - 1)
    def _():
        o_ref[...] = x1_ref[...] + gate_ref[...] * (_rms(acc_ref[...]) * ng_ref[...])


def _ffn_call(h2, w_up, w_down, x1, gate, ng, tm):
    rows = h2.shape[0]
    d_ff = w_down.shape[0]
    tk = 256
    nk = d_ff // tk
    assert d_ff % tk == 0
    per_row = gate.shape[0] != 1
    mod_spec = (pl.BlockSpec((tm, D_MODEL), lambda i, k: (i, 0)) if per_row
                else pl.BlockSpec((1, D_MODEL), lambda i, k: (0, 0)))
    row = pl.BlockSpec((tm, D_MODEL), lambda i, k: (i, 0))
    return pl.pallas_call(
        _ffn_kernel,
        grid=(rows // tm, nk),
        in_specs=[row,
                  pl.BlockSpec((D_MODEL, tk), lambda i, k: (0, k)),
                  pl.BlockSpec((D_MODEL, tk), lambda i, k: (0, nk + k)),
                  pl.BlockSpec((tk, D_MODEL), lambda i, k: (k, 0)),
                  row, mod_spec,
                  pl.BlockSpec((1, D_MODEL), lambda i, k: (0, 0))],
        out_specs=row,
        out_shape=jax.ShapeDtypeStruct((rows, D_MODEL), F32),
        scratch_shapes=[pltpu.VMEM((tm, D_MODEL), F32)],
        compiler_params=_cparams(("arbitrary", "arbitrary")),
        name="ffn",
    )(h2, w_up, w_up, w_down, x1, gate, ng)


def _nsa_s1_kernel(q_ref, kc_ref, vct_ref, oc_ref, idx_ref, *, nt, past_len, n_cmp, n_sel, topk):
    n_cmp_pad = kc_ref.shape[1]
    n_sel_pad = ((n_sel + LANES - 1) // LANES) * LANES
    rows = GROUP_NSA * nt
    q_pos = past_len + lax.broadcasted_iota(I32, (rows, 1), 0) % nt
    n = lax.broadcasted_iota(I32, (rows, n_cmp_pad), 1)
    mask = (n * CMP_STRIDE + CMP_LEN - 1 <= q_pos) & (n < n_cmp)
    probs = []
    for g in range(N_KV_NSA):
        q = q_ref[g * rows:(g + 1) * rows, :]
        p = _masked_softmax(_dot_nt(q, kc_ref[g]), mask)
        oc_ref[g * rows:(g + 1) * rows, :] = _dot_nt(p.astype(BF16), vct_ref[g])
        probs.append(p)
    p_all = jnp.concatenate(probs, axis=0)
    ro = lax.broadcasted_iota(I32, (N_KV_NSA * nt, N_KV_NSA * rows), 0)
    ri = lax.broadcasted_iota(I32, (N_KV_NSA * nt, N_KV_NSA * rows), 1)
    pick = jnp.where((ro // nt == ri // rows) & (ro % nt == ri % nt), 1.0, 0.0).astype(BF16)
    p1, p2, p3 = _split3(p_all)
    p_sum = _dot(pick, p1) + _dot(pick, p2) + _dot(pick, p3)
    imp = _importance(p_sum, n_sel_pad, axis=1)
    t_pos = past_len + lax.broadcasted_iota(I32, (N_KV_NSA * nt, 1), 0) % nt
    score = _block_scores(imp, t_pos, n_sel, axis=1)
    lane = lax.broadcasted_iota(I32, (N_KV_NSA * nt, LANES), 1)
    out = jnp.full((N_KV_NSA * nt, LANES), -1.0, F32)
    for r, (val, idx) in enumerate(_topk_rounds(score, topk, axis=1)):
        out = jnp.where(lane == r, jnp.where(val > 0.5 * NEG_INF, idx, -1.0), out)
    idx_ref[...] = out.astype(I32)


def _nsa_s1_call(q1, kcmp, vcmpt, nt, past_len, n_cmp, n_sel):
    nb = q1.shape[0]
    n_half = kcmp.shape[2]
    kern = functools.partial(_nsa_s1_kernel, nt=nt, past_len=past_len, n_cmp=n_cmp, n_sel=n_sel,
                             topk=min(TOP_N, n_sel))
    rows = N_HEADS_NSA * nt
    per_b = lambda *shape: pl.BlockSpec((None,) + shape, lambda b: (b,) + (0,) * len(shape))
    return pl.pallas_call(
        kern,
        grid=(nb,),
        in_specs=[per_b(rows, HEAD_DIM), per_b(N_KV_NSA, n_half, HEAD_DIM), per_b(N_KV_NSA, HEAD_DIM, n_half)],
        out_specs=(per_b(rows, HEAD_DIM), per_b(N_KV_NSA * nt, LANES)),
        out_shape=(jax.ShapeDtypeStruct((nb, rows, HEAD_DIM), F32),
                   jax.ShapeDtypeStruct((nb, N_KV_NSA * nt, LANES), I32)),
        compiler_params=_cparams(("arbitrary",)),
        name="nsa_sample_topk",
    )(q1, kcmp, vcmpt)


def _nsa_s2_kernel(idx_s, pt_s, q_ref, idxv_ref, oc_ref, gate_ref, newkv_ref, wkv_ref, wnew_ref, cache_hbm,
                   o_ref, kbuf, vbuf, sem, os_scr, *, nt, n_pages, n_past_blocks, topk):
    b = pl.program_id(0)
    ngt = N_KV_NSA * nt
    bpp = PAGE // SEL_BLOCK

    def slot_copies(j, gt, s):
        page = pt_s[b * n_pages + j // bpp]
        dst = lambda buf: buf.at[gt, :, pl.ds(pl.multiple_of(s * PAGE, PAGE), PAGE)]
        return (pltpu.make_async_copy(cache_hbm.at[0, page, 2, gt // nt], dst(kbuf), sem.at[0]),
                pltpu.make_async_copy(cache_hbm.at[0, page, 3, gt // nt], dst(vbuf), sem.at[0]))

    def issue(n, _):
        gt, s = n // topk, n % topk
        j = idx_s[(b * ngt + gt) * topk + s]
        valid = (j >= 0) & (j < n_past_blocks)

        @pl.when(valid)
        def _():
            for cpy in slot_copies(j, gt, s):
                cpy.start()

        @pl.when(jnp.logical_not(valid))
        def _():
            zeros = jnp.zeros((HEAD_DIM, PAGE), F32)
            kbuf[gt, :, pl.ds(pl.multiple_of(s * PAGE, PAGE), PAGE)] = zeros
            vbuf[gt, :, pl.ds(pl.multiple_of(s * PAGE, PAGE), PAGE)] = zeros
        return 0

    def drain(n, _):
        gt, s = n // topk, n % topk
        j = idx_s[(b * ngt + gt) * topk + s]

        @pl.when((j >= 0) & (j < n_past_blocks))
        def _():
            for cpy in slot_copies(j, gt, s):
                cpy.wait()
        return 0

    lax.fori_loop(0, ngt * topk, issue, 0)

    rows = GROUP_NSA * nt
    t_row = lax.broadcasted_iota(I32, (rows, 1), 0) // GROUP_NSA
    nw = wkv_ref.shape[1]
    nn = newkv_ref.shape[1]
    new_ok = lax.broadcasted_iota(I32, (rows, nn), 1) <= t_row
    win_ok = lax.broadcasted_iota(I32, (rows, nw), 1) > t_row
    o_w = []
    for g in range(N_KV_NSA):
        q = q_ref[g * rows:(g + 1) * rows, :]
        s_old = jnp.where(win_ok, _dot_nt(q, wkv_ref[g]), NEG_INF)
        s_new = jnp.where(new_ok, _dot_nt(q, wnew_ref[g]), NEG_INF)
        m = jnp.maximum(jnp.max(s_old, axis=-1, keepdims=True), jnp.max(s_new, axis=-1, keepdims=True))
        e_old = jnp.where(win_ok, jnp.exp2(s_old - m), 0.0)
        e_new = jnp.where(new_ok, jnp.exp2(s_new - m), 0.0)
        l = jnp.sum(e_old, axis=-1, keepdims=True) + jnp.sum(e_new, axis=-1, keepdims=True)
        pv = _dot(e_old.astype(BF16), wkv_ref[g]) + _dot(e_new.astype(BF16), wnew_ref[g])
        o_w.append(pv[:, HEAD_DIM:] / jnp.maximum(l, TINY))
    o_w = jnp.concatenate(o_w, axis=0)

    lax.fori_loop(0, ngt * topk, drain, 0)

    idxv = idxv_ref[...]
    slot_valid = (idxv >= 0) & (idxv < n_past_blocks)
    odd = (idxv & (bpp - 1)) == 1
    es = lax.broadcasted_iota(I32, (LANES, topk * PAGE), 0)
    ek = lax.broadcasted_iota(I32, (LANES, topk * PAGE), 1)
    in_slot = es == ek // PAGE
    key_odd = (ek // SEL_BLOCK) % bpp == 1
    pick = lambda cond: jnp.where(cond, 1.0, 0.0).astype(BF16)
    key_valid = (_dot(pick(slot_valid & jnp.logical_not(odd)), pick(in_slot & jnp.logical_not(key_odd)))
                 + _dot(pick(slot_valid & odd), pick(in_slot & key_odd))) > 0.5
    for gt in range(ngt):
        g, t = gt // nt, gt % nt
        q = q_ref[gt * GROUP_NSA:(gt + 1) * GROUP_NSA, :]
        kvn = newkv_ref[g]
        ok_old = key_valid[gt:gt + 1, :]
        ok_new = lax.broadcasted_iota(I32, (GROUP_NSA, nn), 1) <= t
        s_old = jnp.where(ok_old, _dot(q[:, :HEAD_DIM], kbuf[gt].astype(BF16)), NEG_INF)
        s_new = jnp.where(ok_new, _dot_nt(q, kvn), NEG_INF)
        m = jnp.maximum(jnp.max(s_old, axis=-1, keepdims=True), jnp.max(s_new, axis=-1, keepdims=True))
        e_old = jnp.where(ok_old, jnp.exp2(s_old - m), 0.0)
        e_new = jnp.where(ok_new, jnp.exp2(s_new - m), 0.0)
        l = jnp.sum(e_old, axis=-1, keepdims=True) + jnp.sum(e_new, axis=-1, keepdims=True)
        pv = _dot_nt(e_old.astype(BF16), vbuf[gt].astype(BF16)) + _dot(e_new.astype(BF16), kvn)[:, HEAD_DIM:]
        os_scr[gt * GROUP_NSA:(gt + 1) * GROUP_NSA, :] = pv / jnp.maximum(l, TINY)

    gates = gate_ref[...]
    o_ref[...] = gates[:, 0:1] * oc_ref[...] + gates[:, 1:2] * os_scr[...] + gates[:, 2:3] * o_w


def _nsa_s2_call(idx, page_table, q2, oc, gates, newkv, wkv, wnew, cache, nt, past_len):
    nb, n_pages = page_table.shape
    topk = TOP_N
    ngt = N_KV_NSA * nt
    rows = N_HEADS_NSA * nt
    kern = functools.partial(_nsa_s2_kernel, nt=nt, n_pages=n_pages, n_past_blocks=past_len // SEL_BLOCK, topk=topk)
    per_b = lambda *shape: pl.BlockSpec((None,) + shape, lambda b, i_s, p_s: (b,) + (0,) * len(shape))
    grid_spec = pltpu.PrefetchScalarGridSpec(
        num_scalar_prefetch=2,
        grid=(nb,),
        in_specs=[per_b(rows, LANES), per_b(ngt, LANES), per_b(rows, HEAD_DIM), per_b(rows, LANES),
                  per_b(N_KV_NSA, newkv.shape[2], LANES), per_b(N_KV_NSA, wkv.shape[2], LANES),
                  per_b(N_KV_NSA, wnew.shape[2], LANES), pl.BlockSpec(memory_space=pl.ANY)],
        out_specs=per_b(rows, HEAD_DIM),
        scratch_shapes=[pltpu.VMEM((ngt, HEAD_DIM, topk * PAGE), F32),
                        pltpu.VMEM((ngt, HEAD_DIM, topk * PAGE), F32),
                        pltpu.SemaphoreType.DMA((1,)),
                        pltpu.VMEM((rows, HEAD_DIM), F32)],
    )
    return pl.pallas_call(
        kern, grid_spec=grid_spec,
        out_shape=jax.ShapeDtypeStruct((nb, rows, HEAD_DIM), F32),
        compiler_params=_cparams(("arbitrary",)),
        name="nsa_sample_gather",
    )(idx[:, :, :topk].reshape(-1), page_table.reshape(-1), q2, idx, oc, gates, newkv, wkv, wnew, cache)


def _sb_sample_kernel(pt_s, qbd_ref, newk_ref, newv_ref, cache_hbm, o_ref, buf, sem, *, nt, n_pages):
    b = pl.program_id(0)
    nh = N_HEADS_SB
    rows = nt * nh
    qbd = qbd_ref[...]
    own_head = (lax.broadcasted_iota(I32, (rows, Q_SB), 0) % nh
                == lax.broadcasted_iota(I32, (rows, Q_SB), 1) // HEAD_DIM)

    def step(kt, vt, mask, carry, acc):
        a, carry = _sb_weights(_dot(qbd, kt), mask, carry)
        return carry, acc + jnp.where(own_head, _dot_nt(a.astype(BF16), vt), 0.0)

    def copy(pg, slot):
        return pltpu.make_async_copy(cache_hbm.at[0, pt_s[b * n_pages + pg]], buf.at[slot], sem.at[slot])

    copy(n_pages - 1, 0).start()
    t_row = lax.broadcasted_iota(I32, (rows, PAGE), 0) // nh
    new_mask = lax.broadcasted_iota(I32, (rows, PAGE), 1) < t_row
    carry, acc = step(newk_ref[...], newv_ref[...], new_mask,
                      jnp.zeros((rows, 1), F32), jnp.zeros((rows, Q_SB), F32))

    def cond(st):
        return (st[0] >= 0) & (st[4] > SB_EXIT)

    def body(st):
        pg, slot, carry, acc, _ = st
        copy(pg, slot).wait()

        @pl.when(pg > 0)
        def _():
            copy(pg - 1, 1 - slot).start()

        kt = buf[slot, 0].reshape(Q_SB, PAGE).astype(BF16)
        vt = buf[slot, 1].reshape(Q_SB, PAGE).astype(BF16)
        carry, acc = step(kt, vt, None, carry, acc)
        return pg - 1, 1 - slot, carry, acc, jnp.max(carry)

    pg, slot, _, acc, _ = lax.while_loop(cond, body, (jnp.int32(n_pages - 1), jnp.int32(0), carry, acc, jnp.max(carry)))

    @pl.when(pg >= 0)
    def _():
        copy(pg, slot).wait()

    o_ref[...] = jnp.sum(acc.reshape(nt, nh, Q_SB), axis=1)


def _sb_sample_call(page_table, qbd, newk, newv, cache, nt):
    nb, n_pages = page_table.shape
    nh = N_HEADS_SB
    kern = functools.partial(_sb_sample_kernel, nt=nt, n_pages=n_pages)
    per_b = lambda *shape: pl.BlockSpec((None,) + shape, lambda b, p_s: (b,) + (0,) * len(shape))
    grid_spec = pltpu.PrefetchScalarGridSpec(
        num_scalar_prefetch=1,
        grid=(nb,),
        in_specs=[per_b(nt * nh, Q_SB), per_b(Q_SB, PAGE), per_b(Q_SB, PAGE), pl.BlockSpec(memory_space=pl.ANY)],
        out_specs=per_b(nt, Q_SB),
        scratch_shapes=[pltpu.VMEM((2, 2, nh, HEAD_DIM, PAGE), F32), pltpu.SemaphoreType.DMA((2,))],
    )
    return pl.pallas_call(
        kern, grid_spec=grid_spec,
        out_shape=jax.ShapeDtypeStruct((nb, nt, Q_SB), F32),
        compiler_params=_cparams(("arbitrary",)),
        name="sb_sample",
    )(page_table.reshape(-1), qbd, newk, newv, cache)


def _rope_tables(pos):
    half = ROPE_DIM // 2
    inv_freq = ROPE_THETA ** (-jnp.arange(half, dtype=F32) / half)
    ang = pos.astype(F32)[:, None] * inv_freq[None, :]
    cos, sin = jnp.cos(ang), jnp.sin(ang)
    one = jnp.ones((pos.shape[0], HEAD_DIM - ROPE_DIM), F32)
    c64 = jnp.concatenate([cos, cos, one], axis=1)
    s64 = jnp.concatenate([-sin, sin, jnp.zeros_like(one)], axis=1)
    return jnp.tile(c64, (1, LANES // HEAD_DIM)), jnp.tile(s64, (1, LANES // HEAD_DIM))


def _pad_w_in(w_in):
    c_ga = 512 + 768
    n_ga = 3 * N_HEADS_NSA
    ga = w_in[:, c_ga:c_ga + n_ga].reshape(D_MODEL, N_KV_NSA, n_ga // N_KV_NSA)
    ga = jnp.pad(ga, ((0, 0), (0, 0), (0, LANES - n_ga // N_KV_NSA))).reshape(D_MODEL, N_KV_NSA * LANES)
    return jnp.concatenate([w_in[:, :c_ga], ga, w_in[:, c_ga + n_ga:]], axis=1).astype(BF16)


def _pad_axis(x, axis, size):
    pad = [(0, 0)] * x.ndim
    pad[axis] = (0, size - x.shape[axis])
    return jnp.pad(x, pad)


def kernel(x_prompt, x_sample, cache_nsa_kv, cache_sb_kv, state_win_kv, page_table, c_prompt, c_sample, ada_w, ada_b, norm_g, w_in, cmp_pe, cmp_w1, cmp_b1, cmp_w2, cmp_b2, w_o_nsa, w_o_sb, w_out, ffn_w_up, ffn_w_down):
    assert ada_w.shape[0] == 1 and x_prompt.shape[0] == 1
    t = x_prompt.shape[1]
    nb, nt = x_sample.shape[:2]
    n_pages = page_table.shape[1]
    past_len = n_pages * PAGE
    assert nt <= 8 and state_win_kv.shape[2] == WINDOW and t >= WINDOW

    w = _pad_w_in(w_in[0])
    cw = _compress_weights(cmp_pe[0], cmp_w1[0], cmp_b1[0], cmp_w2[0], cmp_b2[0])
    wa = w_o_nsa[0].astype(BF16)
    wb = w_o_sb[0].astype(BF16)
    wo = w_out[0].astype(BF16)
    wup = ffn_w_up[0].astype(BF16)
    wdn = ffn_w_down[0].astype(BF16)
    ng = norm_g[0]

    c_all = _pad_axis(jnp.concatenate([c_prompt, c_sample], axis=0), 0, -(-(1 + nb) // SUBLANES) * SUBLANES)
    mod = _mod_call(c_all, ada_w[0], ada_b[0][None])
    modp = [mod[0:1, k * D_MODEL:(k + 1) * D_MODEL] for k in range(6)]
    mods = [jnp.repeat(mod[1:1 + nb, k * D_MODEL:(k + 1) * D_MODEL], nt, axis=0) for k in range(6)]

    xp = x_prompt[0]
    cos, sin = _rope_tables(jnp.arange(t))
    (_, qat, nsa_rows, win_rows, kslc, vslct, kwin, vwint, _, gatest, qb, sb_rows, ksb, vsb, mg) = _proj_call(
        xp, modp[0], modp[1], ng[0:1], w, cos, sin, tm=256)
    ident = jnp.arange(t // PAGE, dtype=I32).reshape(1, -1)
    kcmp, vcmpt = _compress_call(ident, nsa_rows.reshape(t // PAGE, PAGE, 512), *cw)
    oa = _nsa_prompt_call(qat, kcmp[0], vcmpt[0], kslc, vslct, kwin, vwint, gatest, n_cmp=t // CMP_STRIDE - 1)
    ob = _sb_prompt_call(qb, ksb, vsb)
    x1, h2 = _merge_call(oa, ob, mg, wa, wb, wo, xp, modp[2], ng[1:2], ng[2:3], modp[3], modp[4], tm=256)
    y_prompt = _ffn_call(h2, wup, wdn, x1, modp[5], ng[3:4], tm=512)

    rs = nb * nt
    xs = x_sample.reshape(rs, D_MODEL)
    cos_s, sin_s = _rope_tables(past_len + jnp.tile(jnp.arange(nt), nb))
    (qa_s, _, nsa_new, win_new, _, _, _, _, gates_s, _, qb_s, sb_new, _, _, mg_s) = _proj_call(
        xs, mods[0], mods[1], ng[0:1], w, cos_s, sin_s, tm=rs)

    nsa_cache = cache_nsa_kv.transpose(0, 1, 3, 4, 5, 2)
    sb_cache = cache_sb_kv.transpose(0, 1, 3, 4, 5, 2)
    kcmp_s, vcmpt_s = _compress_call(page_table, nsa_cache, *cw)
    n_cmp = past_len // CMP_STRIDE - 1
    n_sel = past_len // SEL_BLOCK + 1

    qa5 = qa_s.reshape(nb, nt, N_KV_NSA, GROUP_NSA, HEAD_DIM)
    q1 = qa5.transpose(0, 2, 3, 1, 4).reshape(nb, N_HEADS_NSA * nt, HEAD_DIM)
    oc, idx = _nsa_s1_call(q1, kcmp_s, vcmpt_s, nt, past_len, n_cmp, n_sel)
    q2 = qa5.transpose(0, 2, 1, 3, 4).reshape(nb, N_HEADS_NSA * nt, HEAD_DIM)
    q2 = _pad_axis(q2, 2, LANES)
    oc2 = oc.reshape(nb, N_KV_NSA, GROUP_NSA, nt, HEAD_DIM).transpose(0, 1, 3, 2, 4).reshape(nb, -1, HEAD_DIM)
    g2 = gates_s.reshape(nb, nt, N_KV_NSA, LANES)[..., :3 * GROUP_NSA].reshape(nb, nt, N_KV_NSA, GROUP_NSA, 3)
    g2 = _pad_axis(g2.transpose(0, 2, 1, 3, 4).reshape(nb, N_HEADS_NSA * nt, 3), 2, LANES)

    def new_rows_kv(rows_kv):
        kv = rows_kv.reshape(nb, nt, 2, N_KV_NSA, HEAD_DIM).transpose(0, 3, 1, 2, 4).reshape(nb, N_KV_NSA, nt, LANES)
        return _pad_axis(kv, 2, SUBLANES).astype(BF16)

    newkv = new_rows_kv(nsa_new[:, 256:512])
    wnew = new_rows_kv(win_new)
    wkv = state_win_kv[0].transpose(0, 3, 1, 2, 4).reshape(nb, N_KV_NSA, WINDOW, LANES).astype(BF16)
    oa_s = _nsa_s2_call(idx, page_table, q2, oc2, g2, newkv, wkv, wnew, nsa_cache, nt, past_len)
    oa_s = oa_s.reshape(nb, N_KV_NSA, nt, GROUP_NSA, HEAD_DIM).transpose(0, 2, 1, 3, 4).reshape(rs, Q_NSA).astype(BF16)

    qb4 = qb_s.reshape(nb, nt, N_HEADS_SB, 1, HEAD_DIM)
    eye = jnp.eye(N_HEADS_SB, dtype=BF16)[None, None, :, :, None]
    qbd = (qb4 * eye).reshape(nb, nt * N_HEADS_SB, Q_SB)
    new_sb = sb_new.reshape(nb, nt, 2, N_HEADS_SB, HEAD_DIM).transpose(2, 0, 3, 4, 1)
    new_sb = _pad_axis(new_sb, 4, PAGE).reshape(2, nb, Q_SB, PAGE).astype(BF16)
    ob_s = _sb_sample_call(page_table, qbd, new_sb[0], new_sb[1], sb_cache, nt)
    ob_s = ob_s.reshape(rs, Q_SB).astype(BF16)

    x1s, h2s = _merge_call(oa_s, ob_s, mg_s, wa, wb, wo, xs, mods[2], ng[1:2], ng[2:3], mods[3], mods[4], tm=rs)
    y_sample = _ffn_call(h2s, wup, wdn, x1s, mods[5], ng[3:4], tm=rs)

    win_new5 = win_new.reshape(nb, nt, 2, N_KV_NSA, HEAD_DIM)
    win_kv_sample = jnp.concatenate([state_win_kv[0], win_new5], axis=1)[:, nt:][:, -WINDOW:]
    return (y_prompt[None],
            y_sample.reshape(nb, nt, D_MODEL),
            nsa_rows.reshape(1, 1, t, 4, N_KV_NSA, HEAD_DIM),
            sb_rows.reshape(1, 1, t, 2, N_HEADS_SB, HEAD_DIM),
            win_rows[t - WINDOW:].reshape(1, 1, WINDOW, 2, N_KV_NSA, HEAD_DIM),
            nsa_new.reshape(1, nb, nt, 4, N_KV_NSA, HEAD_DIM),
            sb_new.reshape(1, nb, nt, 2, N_HEADS_SB, HEAD_DIM),
            win_kv_sample[None])
```

```python
import functools

import jax
import jax.numpy as jnp
from jax import lax
from jax.experimental import pallas as pl
from jax.experimental.pallas import tpu as pltpu

F32 = jnp.float32
BF16 = jnp.bfloat16
I32 = jnp.int32

D_MODEL = 1024
HEAD_DIM = 64
N_HEADS_NSA = 8
N_KV_NSA = 2
GROUP_NSA = N_HEADS_NSA // N_KV_NSA
N_HEADS_SB = 8
ROPE_DIM = HEAD_DIM // 4
ROPE_THETA = 500000.0
CMP_STRIDE = 16
CMP_LEN = 2 * CMP_STRIDE
CMP_HIDDEN = 4 * HEAD_DIM
SEL_BLOCK = 64
SEL_RATIO = SEL_BLOCK // CMP_STRIDE
TOP_N = 16
WINDOW = 512
PAGE = 128
RMS_EPS = 1e-6
NEG_INF = -1e30
BIG = 1e30
TINY = 1e-30
ATTN_SCALE = HEAD_DIM ** -0.5
LOG2_E = 1.4426950408889634

LANES = 128
SUBLANES = 8
VMEM_LIMIT = 56 * 1024 * 1024
SB_EXIT = -104.0

C_QA = 0
C_KVA = 512
C_GA = 1280
C_QB = 1536
C_KVB = 2048
C_MG = 3072
C_END = 5120

Q_NSA = N_HEADS_NSA * HEAD_DIM
Q_SB = N_HEADS_SB * HEAD_DIM
QG = GROUP_NSA * HEAD_DIM


def _cparams(sem):
    return pltpu.CompilerParams(dimension_semantics=sem, vmem_limit_bytes=VMEM_LIMIT)


def _rms(x):
    return x * lax.rsqrt(jnp.mean(x * x, axis=-1, keepdims=True) + RMS_EPS)


def _dot(a, b):
    return jnp.dot(a, b, preferred_element_type=F32)


def _dot_nt(a, b):
    return lax.dot_general(a, b, (((1,), (1,)), ((), ())), preferred_element_type=F32)


def _split3(x):
    p1 = x.astype(BF16)
    r1 = x - p1.astype(F32)
    p2 = r1.astype(BF16)
    p3 = (r1 - p2.astype(F32)).astype(BF16)
    return p1, p2, p3


def _softplus(z):
    return jnp.maximum(z, 0.0) + jnp.log(1.0 + jnp.exp(-jnp.abs(z)))


def _mod_kernel(c_ref, w_ref, b_ref, o_ref):
    c = c_ref[...]
    s = c * jax.nn.sigmoid(c)
    o_ref[...] = _dot(s.astype(BF16), w_ref[...].astype(BF16)) + b_ref[...]


def _mod_call(c, w, b):
    rows, n = c.shape[0], w.shape[1]
    tn = 1536
    return pl.pallas_call(
        _mod_kernel,
        grid=(n // tn,),
        in_specs=[pl.BlockSpec((rows, D_MODEL), lambda j: (0, 0)),
                  pl.BlockSpec((D_MODEL, tn), lambda j: (0, j)),
                  pl.BlockSpec((1, tn), lambda j: (0, j))],
        out_specs=pl.BlockSpec((rows, tn), lambda j: (0, j)),
        out_shape=jax.ShapeDtypeStruct((rows, n), F32),
        compiler_params=_cparams(("arbitrary",)),
        name="adaln_mod",
    )(c, w, b)


def _rope(y, cos, sin):
    w = y.shape[1]
    reps = w // LANES
    if reps > 1:
        cos = jnp.concatenate([cos] * reps, axis=1)
        sin = jnp.concatenate([sin] * reps, axis=1)
    lane = lax.broadcasted_iota(I32, y.shape, 1) & (HEAD_DIM - 1)
    half = ROPE_DIM // 2
    partner = jnp.where(lane < half, pltpu.roll(y, w - half, 1), pltpu.roll(y, half, 1))
    return y * cos + partner * sin


def _proj_kernel(x_ref, shift_ref, scale_ref, ng_ref, w_ref, cos_ref, sin_ref,
                 qa_ref, qat_ref, nsa_ref, win_ref, kslc_ref, vslct_ref, kwin_ref, vwint_ref, gate_ref, gatet_ref,
                 qb_ref, sb_ref, ksb_ref, vsb_ref, mg_ref):
    x = x_ref[...]
    h = _rms(x) * ng_ref[...]
    h = h * (1.0 + scale_ref[...]) + shift_ref[...]
    hb = h.astype(BF16)
    cos = cos_ref[...]
    sin = sin_ref[...]

    def mm(c0, c1):
        return _dot(hb, w_ref[:, c0:c1])

    def heads(dst, y, n):
        for i in range(n):
            dst[i] = y[:, i * HEAD_DIM:(i + 1) * HEAD_DIM].astype(dst.dtype)

    def heads_t(dst, y, n):
        yt = y.T
        for i in range(n):
            dst[i] = yt[i * HEAD_DIM:(i + 1) * HEAD_DIM, :].astype(dst.dtype)

    qa = _rope(mm(C_QA, C_QA + Q_NSA), cos, sin) * (ATTN_SCALE * LOG2_E)
    qa_ref[...] = qa.astype(BF16)
    heads_t(qat_ref, qa, N_HEADS_NSA)

    kc = _rope(mm(C_KVA, C_KVA + 128), cos, sin)
    vc = mm(C_KVA + 128, C_KVA + 256)
    ks = _rope(mm(C_KVA + 256, C_KVA + 384), cos, sin)
    vs = mm(C_KVA + 384, C_KVA + 512)
    kw = _rope(mm(C_KVA + 512, C_KVA + 640), cos, sin)
    vw = mm(C_KVA + 640, C_KVA + 768)
    nsa_ref[:, 0:128] = kc
    nsa_ref[:, 128:256] = vc
    nsa_ref[:, 256:384] = ks
    nsa_ref[:, 384:512] = vs
    win_ref[:, 0:128] = kw
    win_ref[:, 128:256] = vw
    heads(kslc_ref, ks, N_KV_NSA)
    heads_t(vslct_ref, vs, N_KV_NSA)
    heads(kwin_ref, kw, N_KV_NSA)
    heads_t(vwint_ref, vw, N_KV_NSA)

    gates = jax.nn.sigmoid(mm(C_GA, C_GA + 2 * LANES))
    gate_ref[...] = gates
    gatet_ref[...] = gates.T

    qb_ref[...] = (mm(C_QB, C_QB + Q_SB) * ATTN_SCALE).astype(BF16)

    kb = mm(C_KVB, C_KVB + Q_SB)
    vb = mm(C_KVB + Q_SB, C_KVB + 2 * Q_SB)
    sb_ref[:, 0:Q_SB] = kb
    sb_ref[:, Q_SB:2 * Q_SB] = vb
    ksb_ref[...] = kb.astype(BF16)
    vsb_ref[...] = vb.astype(BF16)

    for j in range(2):
        mg_ref[:, j * D_MODEL:(j + 1) * D_MODEL] = jax.nn.sigmoid(
            mm(C_MG + j * D_MODEL, C_MG + (j + 1) * D_MODEL)).astype(BF16)


def _proj_call(x, shift, scale, ng, w, cos, sin, tm):
    rows = x.shape[0]
    per_row = shift.shape[0] != 1
    mod_spec = (pl.BlockSpec((tm, D_MODEL), lambda i: (i, 0)) if per_row
                else pl.BlockSpec((1, D_MODEL), lambda i: (0, 0)))
    row = lambda n: pl.BlockSpec((tm, n), lambda i: (i, 0))
    hm = lambda n: pl.BlockSpec((n, tm, HEAD_DIM), lambda i: (0, i, 0))
    hmt = lambda n: pl.BlockSpec((n, HEAD_DIM, tm), lambda i: (0, 0, i))
    sds = jax.ShapeDtypeStruct
    out_shape = (
        sds((rows, Q_NSA), BF16),
        sds((N_HEADS_NSA, HEAD_DIM, rows), BF16),
        sds((rows, 512), F32),
        sds((rows, 256), F32),
        sds((N_KV_NSA, rows, HEAD_DIM), BF16),
        sds((N_KV_NSA, HEAD_DIM, rows), BF16),
        sds((N_KV_NSA, rows, HEAD_DIM), BF16),
        sds((N_KV_NSA, HEAD_DIM, rows), BF16),
        sds((rows, 2 * LANES), F32),
        sds((2 * LANES, rows), F32),
        sds((rows, Q_SB), BF16),
        sds((rows, 2 * Q_SB), F32),
        sds((rows, Q_SB), BF16),
        sds((rows, Q_SB), BF16),
        sds((rows, 2 * D_MODEL), BF16),
    )
    out_specs = (row(Q_NSA), hmt(8), row(512), row(256), hm(2), hmt(2), hm(2), hmt(2), row(2 * LANES),
                 pl.BlockSpec((2 * LANES, tm), lambda i: (0, i)),
                 row(Q_SB), row(2 * Q_SB), row(Q_SB), row(Q_SB), row(2 * D_MODEL))
    return pl.pallas_call(
        _proj_kernel,
        grid=(rows // tm,),
        in_specs=[row(D_MODEL), mod_spec, mod_spec,
                  pl.BlockSpec((1, D_MODEL), lambda i: (0, 0)),
                  pl.BlockSpec((D_MODEL, C_END), lambda i: (0, 0)),
                  row(LANES), row(LANES)],
        out_specs=out_specs,
        out_shape=out_shape,
        compiler_params=_cparams(("arbitrary",)),
        name="proj_in",
    )(x, shift, scale, ng, w, cos, sin)


def _compress_kernel(pt_ref, pages_hbm, w1_ref, bias_ref, w2_ref, b2_ref, kc_ref, vct_ref,
                     buf, stage, sem, hlo, hhi, *, n_pages, cp, token_minor):
    b = pl.program_id(0)
    nch = n_pages // cp
    nh = cp * (PAGE // CMP_STRIDE)

    def copies(page, slot, k):
        if token_minor:
            return [pltpu.make_async_copy(pages_hbm.at[0, page, pl.ds(0, 2)], stage.at[slot, k], sem.at[slot])]
        return [pltpu.make_async_copy(pages_hbm.at[page, :, pl.ds(c * LANES, LANES)],
                                      buf.at[slot, c, pl.ds(k * PAGE, PAGE)], sem.at[slot]) for c in range(2)]

    def start(ch, slot):
        for k in range(cp):
            for cpy in copies(pt_ref[b * n_pages + ch * cp + k], slot, k):
                cpy.start()

    def wait(slot):
        for k in range(cp):
            for cpy in copies(0, slot, k):
                cpy.wait()

    def to_token_major(slot):
        for k in range(cp):
            for c in range(2):
                m = stage[slot, k, c].reshape(N_KV_NSA * HEAD_DIM, PAGE)
                buf[slot, c, pl.ds(k * PAGE, PAGE), :] = m.T

    low = lax.broadcasted_iota(I32, (nh, LANES), 1) < HEAD_DIM

    def regroup(a, b_):
        return (jnp.where(low, a, pltpu.roll(b_, HEAD_DIM, 1)), jnp.where(low, pltpu.roll(a, HEAD_DIM, 1), b_))

    def first_layer(ch, slot):
        row0 = ch * nh if isinstance(ch, int) else pl.multiple_of(ch * nh, nh)
        for c in range(2):
            accs = [jnp.zeros((nh, 2 * CMP_HIDDEN), F32) for _ in range(N_KV_NSA)]
            for pq in range(CMP_STRIDE // 4):
                xp = [buf[slot, c, pl.ds(4 * pq + j, nh, stride=CMP_STRIDE), :] for j in range(4)]
                ab, cd = regroup(xp[0], xp[1]), regroup(xp[2], xp[3])
                for g in range(N_KV_NSA):
                    x4 = jnp.concatenate([ab[g], cd[g]], axis=1).astype(BF16)
                    accs[g] = accs[g] + _dot(x4, w1_ref[c, pq])
            for g in range(N_KV_NSA):
                cg = c * N_KV_NSA + g
                hlo[pl.ds(row0, nh), cg * CMP_HIDDEN:(cg + 1) * CMP_HIDDEN] = accs[g][:, :CMP_HIDDEN]
                hhi[pl.ds(row0, nh), cg * CMP_HIDDEN:(cg + 1) * CMP_HIDDEN] = accs[g][:, CMP_HIDDEN:]

    start(0, 0)
    if token_minor:
        wait(0)
        if nch > 1:
            start(1, 1)
        to_token_major(0)

        def pipelined(ch, _):
            slot = ch % 2
            wait(1 - slot)

            @pl.when(ch + 2 < nch)
            def _():
                start(ch + 2, slot)

            to_token_major(1 - slot)
            first_layer(ch, slot)
            return 0

        lax.fori_loop(0, nch - 1, pipelined, 0)
        first_layer(nch - 1, (nch - 1) % 2)
    else:
        for ch in range(nch):
            slot = ch % 2
            if ch + 1 < nch:
                start(ch + 1, 1 - slot)
            wait(slot)
            first_layer(ch, slot)
    n_half = nch * nh
    hhi[pl.ds(n_half, SUBLANES), :] = jnp.zeros((SUBLANES, 4 * CMP_HIDDEN), F32)
    pre = hlo[...] + hhi[pl.ds(1, n_half), :] + bias_ref[...]
    act = jax.nn.gelu(pre).astype(BF16)
    out = _dot(act, w2_ref[...]) + b2_ref[...]
    out_t = out.T
    for g in range(N_KV_NSA):
        kc_ref[g] = out[:, g * HEAD_DIM:(g + 1) * HEAD_DIM].astype(BF16)
        vct_ref[g] = out_t[(2 + g) * HEAD_DIM:(3 + g) * HEAD_DIM, :].astype(BF16)


def _compress_call(page_table, pages, w1, bias, w2, b2):
    nb, n_pages = page_table.shape
    cp = min(32, n_pages)
    assert n_pages % cp == 0
    n_half = n_pages * (PAGE // CMP_STRIDE)
    token_minor = pages.ndim == 6
    stage_shape = (2, cp, 2, N_KV_NSA, HEAD_DIM, PAGE) if token_minor else (SUBLANES, LANES)
    kern = functools.partial(_compress_kernel, n_pages=n_pages, cp=cp, token_minor=token_minor)
    const = lambda shape: pl.BlockSpec(shape, lambda b, pt: (0,) * len(shape))
    grid_spec = pltpu.PrefetchScalarGridSpec(
        num_scalar_prefetch=1,
        grid=(nb,),
        in_specs=[pl.BlockSpec(memory_space=pl.ANY),
                  const((2, CMP_STRIDE // 4, 4 * HEAD_DIM, 2 * CMP_HIDDEN)),
                  const((1, 4 * CMP_HIDDEN)),
                  const((4 * CMP_HIDDEN, 4 * HEAD_DIM)),
                  const((1, 4 * HEAD_DIM))],
        out_specs=(pl.BlockSpec((None, N_KV_NSA, n_half, HEAD_DIM), lambda b, pt: (b, 0, 0, 0)),
                   pl.BlockSpec((None, N_KV_NSA, HEAD_DIM, n_half), lambda b, pt: (b, 0, 0, 0))),
        scratch_shapes=[pltpu.VMEM((2, 2, cp * PAGE, LANES), F32),
                        pltpu.VMEM(stage_shape, F32),
                        pltpu.SemaphoreType.DMA((2,)),
                        pltpu.VMEM((n_half, 4 * CMP_HIDDEN), F32),
                        pltpu.VMEM((n_half + SUBLANES, 4 * CMP_HIDDEN), F32)],
    )
    return pl.pallas_call(
        kern, grid_spec=grid_spec,
        out_shape=(jax.ShapeDtypeStruct((nb, N_KV_NSA, n_half, HEAD_DIM), BF16),
                   jax.ShapeDtypeStruct((nb, N_KV_NSA, HEAD_DIM, n_half), BF16)),
        compiler_params=_cparams(("arbitrary",)),
        name="compress_kv",
    )(page_table.reshape(-1), pages, w1, bias, w2, b2)


def _compress_weights(cmp_pe, cmp_w1, cmp_b1, cmp_w2, cmp_b2):
    w1r = cmp_w1.reshape(2, CMP_LEN, HEAD_DIM, CMP_HIDDEN)
    w1 = jnp.concatenate([w1r[:, :CMP_STRIDE], w1r[:, CMP_STRIDE:]], axis=-1).astype(BF16)
    w1 = w1.reshape(2, CMP_STRIDE // 4, 4 * HEAD_DIM, 2 * CMP_HIDDEN)
    bias = jnp.einsum('cpd,cpdh->ch', cmp_pe, w1r, precision=lax.Precision.HIGHEST) + cmp_b1
    bias = jnp.repeat(bias, N_KV_NSA, axis=0).reshape(1, 4 * CMP_HIDDEN)
    w2 = jnp.zeros((4, CMP_HIDDEN, 4, HEAD_DIM), F32)
    for cg in range(4):
        w2 = w2.at[cg, :, cg, :].set(cmp_w2[cg // 2])
    w2 = w2.reshape(4 * CMP_HIDDEN, 4 * HEAD_DIM).astype(BF16)
    b2 = jnp.repeat(cmp_b2, N_KV_NSA, axis=0).reshape(1, 4 * HEAD_DIM)
    return w1, bias, w2, b2


def _masked_softmax(s, mask, axis=-1):
    s = jnp.where(mask, s, NEG_INF)
    m = jnp.max(s, axis=axis, keepdims=True)
    e = jnp.where(mask, jnp.exp2(s - m), 0.0)
    return e / jnp.maximum(jnp.sum(e, axis=axis, keepdims=True), TINY)


def _tap_matrix(n_cmp_pad, n_sel_pad, axis):
    shape = (n_cmp_pad, n_sel_pad) if axis == 1 else (n_sel_pad, n_cmp_pad)
    n = lax.broadcasted_iota(I32, shape, 1 - axis)
    j = lax.broadcasted_iota(I32, shape, axis)
    front = CMP_LEN // CMP_STRIDE - 1
    return jnp.where((n >= SEL_RATIO * j - front) & (n <= SEL_RATIO * j + SEL_RATIO - 1), 1.0, 0.0).astype(BF16)


def _importance(p_sum, n_sel_pad, axis):
    taps = _tap_matrix(p_sum.shape[axis], n_sel_pad, axis)
    p1, p2, p3 = _split3(p_sum)
    if axis == 1:
        return _dot(p1, taps) + _dot(p2, taps) + _dot(p3, taps)
    return _dot(taps, p1) + _dot(taps, p2) + _dot(taps, p3)


def _topk_rounds(score, k, axis):
    n = score.shape[axis]
    blk = lax.broadcasted_iota(I32, score.shape, axis).astype(F32)
    outs = []
    work = score
    for _ in range(k):
        mx = jnp.max(work, axis=axis, keepdims=True)
        idx = jnp.min(jnp.where(work == mx, blk, float(n)), axis=axis, keepdims=True)
        outs.append((mx, idx))
        work = jnp.where(blk == idx, -jnp.inf, work)
    return outs


def _block_scores(imp, q_pos, n_sel, axis):
    blk = lax.broadcasted_iota(I32, imp.shape, axis)
    cur = q_pos // SEL_BLOCK
    forced = (blk == 0) | (blk == cur) | (blk == cur - 1)
    score = jnp.where(forced, BIG, jnp.where(blk <= cur, imp, NEG_INF))
    return jnp.where(blk < n_sel, score, -jnp.inf)


def _nsa_prompt_kernel(qt_ref, kc_ref, vct_ref, ks_ref, vst_ref, kw_ref, vwt_ref, gatet_ref, o_ref, bias_scr,
                       *, qb, kb, n_cmp, n_sel, topk):
    i = pl.program_id(1)
    n_cmp_pad = kc_ref.shape[0]
    nq = GROUP_NSA * qb
    qt = jnp.concatenate([qt_ref[r] for r in range(GROUP_NSA)], axis=1)
    q_pos1 = i * qb + lax.broadcasted_iota(I32, (1, qb), 1)
    q_pos = jnp.concatenate([q_pos1] * GROUP_NSA, axis=1)

    s = _dot(kc_ref[...], qt)
    n = lax.broadcasted_iota(I32, (n_cmp_pad, nq), 0)
    p_c = _masked_softmax(s, (n * CMP_STRIDE + CMP_LEN - 1 <= q_pos) & (n < n_cmp), axis=0)
    o_c = _dot(vct_ref[...], p_c.astype(BF16))

    p_sum = p_c[:, 0:qb]
    for r in range(1, GROUP_NSA):
        p_sum = p_sum + p_c[:, r * qb:(r + 1) * qb]
    imp = _importance(p_sum, n_sel, axis=0)
    score = _block_scores(imp, q_pos1, n_sel, axis=0)
    blk = lax.broadcasted_iota(I32, (n_sel, qb), 0).astype(F32)
    bias = jnp.full((n_sel, qb), NEG_INF, F32)
    for val, idx in _topk_rounds(score, topk, axis=0):
        bias = jnp.where((blk == idx) & (val > 0.5 * NEG_INF), 0.0, bias)
    bias_scr[...] = bias

    bpk = kb // SEL_BLOCK
    nkb = (i * qb + qb + kb - 1) // kb

    def sel_step(j, carry, causal):
        m, l, acc = carry
        start = pl.multiple_of(j * kb, kb)
        s = _dot(ks_ref[pl.ds(start, kb), :], qt)
        b8 = bias_scr[pl.ds(pl.multiple_of(j * bpk, bpk), bpk), :]
        b1 = jnp.broadcast_to(b8[:, None, :], (bpk, SEL_BLOCK, qb)).reshape(kb, qb)
        s = s + jnp.concatenate([b1] * GROUP_NSA, axis=1)
        if causal:
            k_pos = start + lax.broadcasted_iota(I32, (kb, nq), 0)
            s = jnp.where(k_pos <= q_pos, s, NEG_INF)
        m_new = jnp.maximum(m, jnp.max(s, axis=0, keepdims=True))
        e = jnp.exp2(s - m_new)
        alpha = jnp.exp2(m - m_new)
        l = alpha * l + jnp.sum(e, axis=0, keepdims=True)
        acc = alpha * acc + _dot(vst_ref[:, pl.ds(start, kb)], e.astype(BF16))
        return m_new, l, acc

    carry = (jnp.full((1, nq), NEG_INF, F32), jnp.zeros((1, nq), F32), jnp.zeros((HEAD_DIM, nq), F32))
    npair = (nkb - 1) // 2
    carry = lax.fori_loop(0, npair, lambda j, c: sel_step(2 * j + 1, sel_step(2 * j, c, False), False), carry)
    carry = lax.cond(nkb - 1 > 2 * npair, lambda c: sel_step(nkb - 2, c, False), lambda c: c, carry)
    _, l_s, acc_s = sel_step(nkb - 1, carry, True)
    o_s = acc_s / jnp.maximum(l_s, TINY)

    wlen = WINDOW + qb
    wstart = pl.multiple_of(jnp.maximum(i * qb - WINDOW, 0), qb)
    s = _dot(kw_ref[pl.ds(wstart, wlen), :], qt)
    dist = q_pos - (wstart + lax.broadcasted_iota(I32, (wlen, nq), 0))
    p_w = _masked_softmax(s, (dist >= 0) & (dist < WINDOW), axis=0)
    o_w = _dot(vwt_ref[:, pl.ds(wstart, wlen)], p_w.astype(BF16))

    gt = gatet_ref[...]
    gate = lambda br: jnp.concatenate([gt[3 * r + br:3 * r + br + 1, :] for r in range(GROUP_NSA)], axis=1)
    o = gate(0) * o_c + gate(1) * o_s + gate(2) * o_w
    o = jnp.concatenate([o[:, r * qb:(r + 1) * qb] for r in range(GROUP_NSA)], axis=0)
    o_ref[...] = o.T.astype(o_ref.dtype)


def _nsa_prompt_call(qat, kcmp, vcmpt, kslc, vslct, kwin, vwint, gatest, n_cmp):
    t = qat.shape[2]
    qb, kb = 128, 512
    assert t % kb == 0 and t >= WINDOW + qb
    n_sel = t // SEL_BLOCK
    n_cmp_pad = kcmp.shape[1]
    kern = functools.partial(_nsa_prompt_kernel, qb=qb, kb=kb, n_cmp=n_cmp, n_sel=n_sel, topk=min(TOP_N, n_sel))
    rows = lambda n: pl.BlockSpec((None, n, HEAD_DIM), lambda g, i: (g, 0, 0))
    cols = lambda n: pl.BlockSpec((None, HEAD_DIM, n), lambda g, i: (g, 0, 0))
    return pl.pallas_call(
        kern,
        grid=(N_KV_NSA, t // qb),
        in_specs=[pl.BlockSpec((GROUP_NSA, HEAD_DIM, qb), lambda g, i: (g, 0, i)),
                  rows(n_cmp_pad), cols(n_cmp_pad), rows(t), cols(t), rows(t), cols(t),
                  pl.BlockSpec((LANES, qb), lambda g, i: (g, i))],
        out_specs=pl.BlockSpec((qb, QG), lambda g, i: (i, g)),
        out_shape=jax.ShapeDtypeStruct((t, Q_NSA), BF16),
        scratch_shapes=[pltpu.VMEM((n_sel, qb), F32)],
        compiler_params=_cparams(("arbitrary", "arbitrary")),
        name="nsa_prompt",
    )(qat, kcmp, vcmpt, kslc, vslct, kwin, vwint, gatest)


def _sb_weights(z, mask, carry):
    kb = z.shape[1]
    l1 = -_softplus(z)
    if mask is not None:
        l1 = jnp.where(mask, l1, 0.0)
    sp = lax.broadcasted_iota(I32, (kb, kb), 0)
    sc = lax.broadcasted_iota(I32, (kb, kb), 1)
    later = jnp.where(sp > sc, 1.0, 0.0).astype(BF16)
    hi = l1.astype(BF16)
    lo = (l1 - hi.astype(F32)).astype(BF16)
    after = _dot(hi, later) + _dot(lo, later) + carry
    a = jnp.exp(z + l1 + after)
    if mask is not None:
        a = jnp.where(mask, a, 0.0)
    return a, carry + jnp.sum(l1, axis=-1, keepdims=True)


def _sb_prompt_kernel(q_ref, k_ref, v_ref, o_ref, *, qb, kb, nh):
    i = pl.program_id(1)
    width = nh * HEAD_DIM
    q = q_ref[...]
    own = [lax.broadcasted_iota(I32, (qb, width), 1) // HEAD_DIM == h for h in range(nh)]
    qh = [jnp.where(own[h], q, jnp.zeros_like(q)) for h in range(nh)]
    q_pos = i * qb + lax.broadcasted_iota(I32, (qb, 1), 0)

    def cond(st):
        return (st[0] > 0) & (st[3] > SB_EXIT)

    def body(st):
        hi, carries, acc, _ = st
        start = pl.multiple_of(jnp.maximum(hi - kb, 0), qb)
        k = k_ref[pl.ds(start, kb), :]
        v = v_ref[pl.ds(start, kb), :]
        mask = (start + lax.broadcasted_iota(I32, (qb, kb), 1)) < jnp.minimum(q_pos, hi)
        new_carries = []
        cmax = None
        for h in range(nh):
            a, c = _sb_weights(_dot_nt(qh[h], k), mask, carries[h])
            acc = acc + jnp.where(own[h], _dot(a.astype(BF16), v), 0.0)
            new_carries.append(c)
            cmax = c if cmax is None else jnp.maximum(cmax, c)
        return hi - kb, tuple(new_carries), acc, jnp.max(cmax)

    st = ((i + 1) * qb, tuple(jnp.zeros((qb, 1), F32) for _ in range(nh)), jnp.zeros((qb, width), F32),
          jnp.float32(0.0))
    acc = lax.while_loop(cond, body, st)[2]
    o_ref[...] = acc.astype(o_ref.dtype)


def _sb_prompt_call(qb_all, ksb, vsb):
    t = qb_all.shape[0]
    qb, kb, nh = 128, 384, 4
    assert t % qb == 0 and kb % qb == 0 and t >= kb
    width = nh * HEAD_DIM
    kern = functools.partial(_sb_prompt_kernel, qb=qb, kb=kb, nh=nh)
    whole = pl.BlockSpec((t, width), lambda h, i: (0, h))
    blk = pl.BlockSpec((qb, width), lambda h, i: (i, h))
    return pl.pallas_call(
        kern,
        grid=(N_HEADS_SB // nh, t // qb),
        in_specs=[blk, whole, whole],
        out_specs=blk,
        out_shape=jax.ShapeDtypeStruct((t, Q_SB), BF16),
        compiler_params=_cparams(("arbitrary", "arbitrary")),
        name="sb_prompt",
    )(qb_all, ksb, vsb)


def _merge_kernel(oa_ref, ob_ref, mg_ref, wa_ref, wb_ref, wo_ref, x_ref, gate_ref, ng1_ref, ng2_ref,
                  shift_ref, scale_ref, x1_ref, h2_ref):
    ya = _dot(oa_ref[...], wa_ref[...])
    yb = _dot(ob_ref[...], wb_ref[...])
    y = mg_ref[:, 0:D_MODEL].astype(F32) * ya + mg_ref[:, D_MODEL:2 * D_MODEL].astype(F32) * yb
    o = _dot(y.astype(BF16), wo_ref[...])
    x1 = x_ref[...] + gate_ref[...] * (_rms(o) * ng1_ref[...])
    x1_ref[...] = x1
    h2 = (_rms(x1) * ng2_ref[...]) * (1.0 + scale_ref[...]) + shift_ref[...]
    h2_ref[...] = h2.astype(BF16)


def _merge_call(oa, ob, mg, wa, wb, wo, x, gate, ng1, ng2, shift, scale, tm):
    rows = x.shape[0]
    per_row = gate.shape[0] != 1
    mod_spec = (pl.BlockSpec((tm, D_MODEL), lambda i: (i, 0)) if per_row
                else pl.BlockSpec((1, D_MODEL), lambda i: (0, 0)))
    vec = pl.BlockSpec((1, D_MODEL), lambda i: (0, 0))
    row = lambda n: pl.BlockSpec((tm, n), lambda i: (i, 0))
    wspec = lambda n: pl.BlockSpec((n, D_MODEL), lambda i: (0, 0))
    return pl.pallas_call(
        _merge_kernel,
        grid=(rows // tm,),
        in_specs=[row(Q_NSA), row(Q_SB), row(2 * D_MODEL), wspec(Q_NSA), wspec(Q_SB), wspec(D_MODEL),
                  row(D_MODEL), mod_spec, vec, vec, mod_spec, mod_spec],
        out_specs=(row(D_MODEL), row(D_MODEL)),
        out_shape=(jax.ShapeDtypeStruct((rows, D_MODEL), F32), jax.ShapeDtypeStruct((rows, D_MODEL), BF16)),
        compiler_params=_cparams(("arbitrary",)),
        name="merge_out",
    )(oa, ob, mg, wa, wb, wo, x, gate, ng1, ng2, shift, scale)


def _ffn_kernel(h_ref, wg_ref, wu_ref, wd_ref, x1_ref, gate_ref, ng_ref, o_ref, acc_ref):
    k = pl.program_id(1)

    @pl.when(k == 0)
    def _():
        acc_ref[...] = jnp.zeros_like(acc_ref)

    h = h_ref[...]
    g = _dot(h, wg_ref[...])
    u = _dot(h, wu_ref[...])
    act = (g * jax.nn.sigmoid(g)) * u
    acc_ref[...] += _dot(act.astype(BF16), wd_ref[...])

    @pl.when(k == pl.num_programs(1) - 1)
    def _():
        o_ref[...] = x1_ref[...] + gate_ref[...] * (_rms(acc_ref[...]) * ng_ref[...])


def _ffn_call(h2, w_up, w_down, x1, gate, ng, tm):
    rows = h2.shape[0]
    d_ff = w_down.shape[0]
    tk = 256
    nk = d_ff // tk
    assert d_ff % tk == 0
    per_row = gate.shape[0] != 1
    mod_spec = (pl.BlockSpec((tm, D_MODEL), lambda i, k: (i, 0)) if per_row
                else pl.BlockSpec((1, D_MODEL), lambda i, k: (0, 0)))
    row = pl.BlockSpec((tm, D_MODEL), lambda i, k: (i, 0))
    return pl.pallas_call(
        _ffn_kernel,
        grid=(rows // tm, nk),
        in_specs=[row,
                  pl.BlockSpec((D_MODEL, tk), lambda i, k: (0, k)),
                  pl.BlockSpec((D_MODEL, tk), lambda i, k: (0, nk + k)),
                  pl.BlockSpec((tk, D_MODEL), lambda i, k: (k, 0)),
                  row, mod_spec,
                  pl.BlockSpec((1, D_MODEL), lambda i, k: (0, 0))],
        out_specs=row,
        out_shape=jax.ShapeDtypeStruct((rows, D_MODEL), F32),
        scratch_shapes=[pltpu.VMEM((tm, D_MODEL), F32)],
        compiler_params=_cparams(("arbitrary", "arbitrary")),
        name="ffn",
    )(h2, w_up, w_up, w_down, x1, gate, ng)


def _nsa_s1_kernel(q_ref, kc_ref, vct_ref, oc_ref, idx_ref, *, nt, past_len, n_cmp, n_sel, topk):
    n_cmp_pad = kc_ref.shape[1]
    n_sel_pad = ((n_sel + LANES - 1) // LANES) * LANES
    rows = GROUP_NSA * nt
    q_pos = past_len + lax.broadcasted_iota(I32, (rows, 1), 0) % nt
    n = lax.broadcasted_iota(I32, (rows, n_cmp_pad), 1)
    mask = (n * CMP_STRIDE + CMP_LEN - 1 <= q_pos) & (n < n_cmp)
    probs = []
    for g in range(N_KV_NSA):
        q = q_ref[g * rows:(g + 1) * rows, :]
        p = _masked_softmax(_dot_nt(q, kc_ref[g]), mask)
        oc_ref[g * rows:(g + 1) * rows, :] = _dot_nt(p.astype(BF16), vct_ref[g])
        probs.append(p)
    p_all = jnp.concatenate(probs, axis=0)
    ro = lax.broadcasted_iota(I32, (N_KV_NSA * nt, N_KV_NSA * rows), 0)
    ri = lax.broadcasted_iota(I32, (N_KV_NSA * nt, N_KV_NSA * rows), 1)
    pick = jnp.where((ro // nt == ri // rows) & (ro % nt == ri % nt), 1.0, 0.0).astype(BF16)
    p1, p2, p3 = _split3(p_all)
    p_sum = _dot(pick, p1) + _dot(pick, p2) + _dot(pick, p3)
    imp = _importance(p_sum, n_sel_pad, axis=1)
    t_pos = past_len + lax.broadcasted_iota(I32, (N_KV_NSA * nt, 1), 0) % nt
    score = _block_scores(imp, t_pos, n_sel, axis=1)
    lane = lax.broadcasted_iota(I32, (N_KV_NSA * nt, LANES), 1)
    out = jnp.full((N_KV_NSA * nt, LANES), -1.0, F32)
    for r, (val, idx) in enumerate(_topk_rounds(score, topk, axis=1)):
        out = jnp.where(lane == r, jnp.where(val > 0.5 * NEG_INF, idx, -1.0), out)
    idx_ref[...] = out.astype(I32)


def _nsa_s1_call(q1, kcmp, vcmpt, nt, past_len, n_cmp, n_sel):
    nb = q1.shape[0]
    n_half = kcmp.shape[2]
    kern = functools.partial(_nsa_s1_kernel, nt=nt, past_len=past_len, n_cmp=n_cmp, n_sel=n_sel,
                             topk=min(TOP_N, n_sel))
    rows = N_HEADS_NSA * nt
    per_b = lambda *shape: pl.BlockSpec((None,) + shape, lambda b: (b,) + (0,) * len(shape))
    return pl.pallas_call(
        kern,
        grid=(nb,),
        in_specs=[per_b(rows, HEAD_DIM), per_b(N_KV_NSA, n_half, HEAD_DIM), per_b(N_KV_NSA, HEAD_DIM, n_half)],
        out_specs=(per_b(rows, HEAD_DIM), per_b(N_KV_NSA * nt, LANES)),
        out_shape=(jax.ShapeDtypeStruct((nb, rows, HEAD_DIM), F32),
                   jax.ShapeDtypeStruct((nb, N_KV_NSA * nt, LANES), I32)),
        compiler_params=_cparams(("arbitrary",)),
        name="nsa_sample_topk",
    )(q1, kcmp, vcmpt)


def _nsa_s2_kernel(idx_s, pt_s, q_ref, idxv_ref, oc_ref, gate_ref, newkv_ref, wkv_ref, wnew_ref, cache_hbm,
                   o_ref, kbuf, vbuf, sem, os_scr, *, nt, n_pages, n_past_blocks, topk):
    b = pl.program_id(0)
    nb = pl.num_programs(0)
    ngt = N_KV_NSA * nt
    bpp = PAGE // SEL_BLOCK
    cur = b % 2

    def slot_copies(seq, buf_i, j, gt, s):
        page = pt_s[seq * n_pages + j // bpp]
        dst = lambda buf: buf.at[buf_i, gt, :, pl.ds(pl.multiple_of(s * PAGE, PAGE), PAGE)]
        return (pltpu.make_async_copy(cache_hbm.at[0, page, 2, gt // nt], dst(kbuf), sem.at[buf_i]),
                pltpu.make_async_copy(cache_hbm.at[0, page, 3, gt // nt], dst(vbuf), sem.at[buf_i]))

    def issue(seq, buf_i):
        def one(n, _):
            gt, s = n // topk, n % topk
            j = idx_s[(seq * ngt + gt) * topk + s]
            valid = (j >= 0) & (j < n_past_blocks)

            @pl.when(valid)
            def _():
                for cpy in slot_copies(seq, buf_i, j, gt, s):
                    cpy.start()

            @pl.when(jnp.logical_not(valid))
            def _():
                zeros = jnp.zeros((HEAD_DIM, PAGE), F32)
                kbuf[buf_i, gt, :, pl.ds(pl.multiple_of(s * PAGE, PAGE), PAGE)] = zeros
                vbuf[buf_i, gt, :, pl.ds(pl.multiple_of(s * PAGE, PAGE), PAGE)] = zeros
            return 0
        lax.fori_loop(0, ngt * topk, one, 0)

    def drain(n, _):
        gt, s = n // topk, n % topk
        j = idx_s[(b * ngt + gt) * topk + s]

        @pl.when((j >= 0) & (j < n_past_blocks))
        def _():
            for cpy in slot_copies(b, cur, j, gt, s):
                cpy.wait()
        return 0

    @pl.when(b == 0)
    def _():
        issue(b, cur)

    @pl.when(b + 1 < nb)
    def _():
        issue(b + 1, 1 - cur)

    rows = GROUP_NSA * nt
    t_row = lax.broadcasted_iota(I32, (rows, 1), 0) // GROUP_NSA
    nw = wkv_ref.shape[1]
    nn = newkv_ref.shape[1]
    new_ok = lax.broadcasted_iota(I32, (rows, nn), 1) <= t_row
    win_ok = lax.broadcasted_iota(I32, (rows, nw), 1) > t_row
    o_w = []
    for g in range(N_KV_NSA):
        q = q_ref[g * rows:(g + 1) * rows, :]
        s_old = jnp.where(win_ok, _dot_nt(q, wkv_ref[g]), NEG_INF)
        s_new = jnp.where(new_ok, _dot_nt(q, wnew_ref[g]), NEG_INF)
        m = jnp.maximum(jnp.max(s_old, axis=-1, keepdims=True), jnp.max(s_new, axis=-1, keepdims=True))
        e_old = jnp.where(win_ok, jnp.exp2(s_old - m), 0.0)
        e_new = jnp.where(new_ok, jnp.exp2(s_new - m), 0.0)
        l = jnp.sum(e_old, axis=-1, keepdims=True) + jnp.sum(e_new, axis=-1, keepdims=True)
        pv = _dot(e_old.astype(BF16), wkv_ref[g]) + _dot(e_new.astype(BF16), wnew_ref[g])
        o_w.append(pv[:, HEAD_DIM:] / jnp.maximum(l, TINY))
    o_w = jnp.concatenate(o_w, axis=0)

    lax.fori_loop(0, ngt * topk, drain, 0)

    idxv = idxv_ref[...]
    slot_valid = (idxv >= 0) & (idxv < n_past_blocks)
    odd = (idxv & (bpp - 1)) == 1
    es = lax.broadcasted_iota(I32, (LANES, topk * PAGE), 0)
    ek = lax.broadcasted_iota(I32, (LANES, topk * PAGE), 1)
    in_slot = es == ek // PAGE
    key_odd = (ek // SEL_BLOCK) % bpp == 1
    pick = lambda cond: jnp.where(cond, 1.0, 0.0).astype(BF16)
    key_valid = (_dot(pick(slot_valid & jnp.logical_not(odd)), pick(in_slot & jnp.logical_not(key_odd)))
                 + _dot(pick(slot_valid & odd), pick(in_slot & key_odd))) > 0.5
    for gt in range(ngt):
        g, t = gt // nt, gt % nt
        q = q_ref[gt * GROUP_NSA:(gt + 1) * GROUP_NSA, :]
        kvn = newkv_ref[g]
        ok_old = key_valid[gt:gt + 1, :]
        ok_new = lax.broadcasted_iota(I32, (GROUP_NSA, nn), 1) <= t
        s_old = jnp.where(ok_old, _dot(q[:, :HEAD_DIM], kbuf[cur, gt].astype(BF16)), NEG_INF)
        s_new = jnp.where(ok_new, _dot_nt(q, kvn), NEG_INF)
        m = jnp.maximum(jnp.max(s_old, axis=-1, keepdims=True), jnp.max(s_new, axis=-1, keepdims=True))
        e_old = jnp.where(ok_old, jnp.exp2(s_old - m), 0.0)
        e_new = jnp.where(ok_new, jnp.exp2(s_new - m), 0.0)
        l = jnp.sum(e_old, axis=-1, keepdims=True) + jnp.sum(e_new, axis=-1, keepdims=True)
        pv = _dot_nt(e_old.astype(BF16), vbuf[cur, gt].astype(BF16)) + _dot(e_new.astype(BF16), kvn)[:, HEAD_DIM:]
        os_scr[gt * GROUP_NSA:(gt + 1) * GROUP_NSA, :] = pv / jnp.maximum(l, TINY)

    gates = gate_ref[...]
    o_ref[...] = gates[:, 0:1] * oc_ref[...] + gates[:, 1:2] * os_scr[...] + gates[:, 2:3] * o_w


def _nsa_s2_call(idx, page_table, q2, oc, gates, newkv, wkv, wnew, cache, nt, past_len):
    nb, n_pages = page_table.shape
    topk = TOP_N
    ngt = N_KV_NSA * nt
    rows = N_HEADS_NSA * nt
    kern = functools.partial(_nsa_s2_kernel, nt=nt, n_pages=n_pages, n_past_blocks=past_len // SEL_BLOCK, topk=topk)
    per_b = lambda *shape: pl.BlockSpec((None,) + shape, lambda b, i_s, p_s: (b,) + (0,) * len(shape))
    grid_spec = pltpu.PrefetchScalarGridSpec(
        num_scalar_prefetch=2,
        grid=(nb,),
        in_specs=[per_b(rows, LANES), per_b(ngt, LANES), per_b(rows, HEAD_DIM), per_b(rows, LANES),
                  per_b(N_KV_NSA, newkv.shape[2], LANES), per_b(N_KV_NSA, wkv.shape[2], LANES),
                  per_b(N_KV_NSA, wnew.shape[2], LANES), pl.BlockSpec(memory_space=pl.ANY)],
        out_specs=per_b(rows, HEAD_DIM),
        scratch_shapes=[pltpu.VMEM((2, ngt, HEAD_DIM, topk * PAGE), F32),
                        pltpu.VMEM((2, ngt, HEAD_DIM, topk * PAGE), F32),
                        pltpu.SemaphoreType.DMA((2,)),
                        pltpu.VMEM((rows, HEAD_DIM), F32)],
    )
    return pl.pallas_call(
        kern, grid_spec=grid_spec,
        out_shape=jax.ShapeDtypeStruct((nb, rows, HEAD_DIM), F32),
        compiler_params=_cparams(("arbitrary",)),
        name="nsa_sample_gather",
    )(idx[:, :, :topk].reshape(-1), page_table.reshape(-1), q2, idx, oc, gates, newkv, wkv, wnew, cache)


def _sb_sample_kernel(pt_s, qbd_ref, newk_ref, newv_ref, cache_hbm, o_ref, buf, sem, *, nt, n_pages):
    b = pl.program_id(0)
    nh = N_HEADS_SB
    rows = nt * nh
    qbd = qbd_ref[...]
    own_head = (lax.broadcasted_iota(I32, (rows, Q_SB), 0) % nh
                == lax.broadcasted_iota(I32, (rows, Q_SB), 1) // HEAD_DIM)

    def step(kt, vt, mask, carry, acc):
        a, carry = _sb_weights(_dot(qbd, kt), mask, carry)
        return carry, acc + jnp.where(own_head, _dot_nt(a.astype(BF16), vt), 0.0)

    def copy(pg, slot):
        return pltpu.make_async_copy(cache_hbm.at[0, pt_s[b * n_pages + pg]], buf.at[slot], sem.at[slot])

    copy(n_pages - 1, 0).start()
    t_row = lax.broadcasted_iota(I32, (rows, PAGE), 0) // nh
    new_mask = lax.broadcasted_iota(I32, (rows, PAGE), 1) < t_row
    carry, acc = step(newk_ref[...], newv_ref[...], new_mask,
                      jnp.zeros((rows, 1), F32), jnp.zeros((rows, Q_SB), F32))

    def cond(st):
        return (st[0] >= 0) & (st[4] > SB_EXIT)

    def body(st):
        pg, slot, carry, acc, _ = st
        copy(pg, slot).wait()

        @pl.when(pg > 0)
        def _():
            copy(pg - 1, 1 - slot).start()

        kt = buf[slot, 0].reshape(Q_SB, PAGE).astype(BF16)
        vt = buf[slot, 1].reshape(Q_SB, PAGE).astype(BF16)
        carry, acc = step(kt, vt, None, carry, acc)
        return pg - 1, 1 - slot, carry, acc, jnp.max(carry)

    pg, slot, _, acc, _ = lax.while_loop(cond, body, (jnp.int32(n_pages - 1), jnp.int32(0), carry, acc, jnp.max(carry)))

    @pl.when(pg >= 0)
    def _():
        copy(pg, slot).wait()

    o_ref[...] = jnp.sum(acc.reshape(nt, nh, Q_SB), axis=1)


def _sb_sample_call(page_table, qbd, newk, newv, cache, nt):
    nb, n_pages = page_table.shape
    nh = N_HEADS_SB
    kern = functools.partial(_sb_sample_kernel, nt=nt, n_pages=n_pages)
    per_b = lambda *shape: pl.BlockSpec((None,) + shape, lambda b, p_s: (b,) + (0,) * len(shape))
    grid_spec = pltpu.PrefetchScalarGridSpec(
        num_scalar_prefetch=1,
        grid=(nb,),
        in_specs=[per_b(nt * nh, Q_SB), per_b(Q_SB, PAGE), per_b(Q_SB, PAGE), pl.BlockSpec(memory_space=pl.ANY)],
        out_specs=per_b(nt, Q_SB),
        scratch_shapes=[pltpu.VMEM((2, 2, nh, HEAD_DIM, PAGE), F32), pltpu.SemaphoreType.DMA((2,))],
    )
    return pl.pallas_call(
        kern, grid_spec=grid_spec,
        out_shape=jax.ShapeDtypeStruct((nb, nt, Q_SB), F32),
        compiler_params=_cparams(("arbitrary",)),
        name="sb_sample",
    )(page_table.reshape(-1), qbd, newk, newv, cache)


def _rope_tables(pos):
    half = ROPE_DIM // 2
    inv_freq = ROPE_THETA ** (-jnp.arange(half, dtype=F32) / half)
    ang = pos.astype(F32)[:, None] * inv_freq[None, :]
    cos, sin = jnp.cos(ang), jnp.sin(ang)
    one = jnp.ones((pos.shape[0], HEAD_DIM - ROPE_DIM), F32)
    c64 = jnp.concatenate([cos, cos, one], axis=1)
    s64 = jnp.concatenate([-sin, sin, jnp.zeros_like(one)], axis=1)
    return jnp.tile(c64, (1, LANES // HEAD_DIM)), jnp.tile(s64, (1, LANES // HEAD_DIM))


def _pad_w_in(w_in):
    c_ga = 512 + 768
    n_ga = 3 * N_HEADS_NSA
    ga = w_in[:, c_ga:c_ga + n_ga].reshape(D_MODEL, N_KV_NSA, n_ga // N_KV_NSA)
    ga = jnp.pad(ga, ((0, 0), (0, 0), (0, LANES - n_ga // N_KV_NSA))).reshape(D_MODEL, N_KV_NSA * LANES)
    return jnp.concatenate([w_in[:, :c_ga], ga, w_in[:, c_ga + n_ga:]], axis=1).astype(BF16)


def _pad_axis(x, axis, size):
    pad = [(0, 0)] * x.ndim
    pad[axis] = (0, size - x.shape[axis])
    return jnp.pad(x, pad)


def kernel(x_prompt, x_sample, cache_nsa_kv, cache_sb_kv, state_win_kv, page_table, c_prompt, c_sample, ada_w, ada_b, norm_g, w_in, cmp_pe, cmp_w1, cmp_b1, cmp_w2, cmp_b2, w_o_nsa, w_o_sb, w_out, ffn_w_up, ffn_w_down):
    assert ada_w.shape[0] == 1 and x_prompt.shape[0] == 1
    t = x_prompt.shape[1]
    nb, nt = x_sample.shape[:2]
    n_pages = page_table.shape[1]
    past_len = n_pages * PAGE
    assert nt <= 8 and state_win_kv.shape[2] == WINDOW and t >= WINDOW

    w = _pad_w_in(w_in[0])
    cw = _compress_weights(cmp_pe[0], cmp_w1[0], cmp_b1[0], cmp_w2[0], cmp_b2[0])
    wa = w_o_nsa[0].astype(BF16)
    wb = w_o_sb[0].astype(BF16)
    wo = w_out[0].astype(BF16)
    wup = ffn_w_up[0].astype(BF16)
    wdn = ffn_w_down[0].astype(BF16)
    ng = norm_g[0]

    c_all = _pad_axis(jnp.concatenate([c_prompt, c_sample], axis=0), 0, -(-(1 + nb) // SUBLANES) * SUBLANES)
    mod = _mod_call(c_all, ada_w[0], ada_b[0][None])
    modp = [mod[0:1, k * D_MODEL:(k + 1) * D_MODEL] for k in range(6)]
    mods = [jnp.repeat(mod[1:1 + nb, k * D_MODEL:(k + 1) * D_MODEL], nt, axis=0) for k in range(6)]

    xp = x_prompt[0]
    cos, sin = _rope_tables(jnp.arange(t))
    (_, qat, nsa_rows, win_rows, kslc, vslct, kwin, vwint, _, gatest, qb, sb_rows, ksb, vsb, mg) = _proj_call(
        xp, modp[0], modp[1], ng[0:1], w, cos, sin, tm=256)
    ident = jnp.arange(t // PAGE, dtype=I32).reshape(1, -1)
    kcmp, vcmpt = _compress_call(ident, nsa_rows.reshape(t // PAGE, PAGE, 512), *cw)
    oa = _nsa_prompt_call(qat, kcmp[0], vcmpt[0], kslc, vslct, kwin, vwint, gatest, n_cmp=t // CMP_STRIDE - 1)
    ob = _sb_prompt_call(qb, ksb, vsb)
    x1, h2 = _merge_call(oa, ob, mg, wa, wb, wo, xp, modp[2], ng[1:2], ng[2:3], modp[3], modp[4], tm=256)
    y_prompt = _ffn_call(h2, wup, wdn, x1, modp[5], ng[3:4], tm=512)

    rs = nb * nt
    xs = x_sample.reshape(rs, D_MODEL)
    cos_s, sin_s = _rope_tables(past_len + jnp.tile(jnp.arange(nt), nb))
    (qa_s, _, nsa_new, win_new, _, _, _, _, gates_s, _, qb_s, sb_new, _, _, mg_s) = _proj_call(
        xs, mods[0], mods[1], ng[0:1], w, cos_s, sin_s, tm=rs)

    nsa_cache = cache_nsa_kv.transpose(0, 1, 3, 4, 5, 2)
    sb_cache = cache_sb_kv.transpose(0, 1, 3, 4, 5, 2)
    kcmp_s, vcmpt_s = _compress_call(page_table, nsa_cache, *cw)
    n_cmp = past_len // CMP_STRIDE - 1
    n_sel = past_len // SEL_BLOCK + 1

    qa5 = qa_s.reshape(nb, nt, N_KV_NSA, GROUP_NSA, HEAD_DIM)
    q1 = qa5.transpose(0, 2, 3, 1, 4).reshape(nb, N_HEADS_NSA * nt, HEAD_DIM)
    oc, idx = _nsa_s1_call(q1, kcmp_s, vcmpt_s, nt, past_len, n_cmp, n_sel)
    q2 = qa5.transpose(0, 2, 1, 3, 4).reshape(nb, N_HEADS_NSA * nt, HEAD_DIM)
    q2 = _pad_axis(q2, 2, LANES)
    oc2 = oc.reshape(nb, N_KV_NSA, GROUP_NSA, nt, HEAD_DIM).transpose(0, 1, 3, 2, 4).reshape(nb, -1, HEAD_DIM)
    g2 = gates_s.reshape(nb, nt, N_KV_NSA, LANES)[..., :3 * GROUP_NSA].reshape(nb, nt, N_KV_NSA, GROUP_NSA, 3)
    g2 = _pad_axis(g2.transpose(0, 2, 1, 3, 4).reshape(nb, N_HEADS_NSA * nt, 3), 2, LANES)

    def new_rows_kv(rows_kv):
        kv = rows_kv.reshape(nb, nt, 2, N_KV_NSA, HEAD_DIM).transpose(0, 3, 1, 2, 4).reshape(nb, N_KV_NSA, nt, LANES)
        return _pad_axis(kv, 2, SUBLANES).astype(BF16)

    newkv = new_rows_kv(nsa_new[:, 256:512])
    wnew = new_rows_kv(win_new)
    wkv = state_win_kv[0].transpose(0, 3, 1, 2, 4).reshape(nb, N_KV_NSA, WINDOW, LANES).astype(BF16)
    oa_s = _nsa_s2_call(idx, page_table, q2, oc2, g2, newkv, wkv, wnew, nsa_cache, nt, past_len)
    oa_s = oa_s.reshape(nb, N_KV_NSA, nt, GROUP_NSA, HEAD_DIM).transpose(0, 2, 1, 3, 4).reshape(rs, Q_NSA).astype(BF16)

    qb4 = qb_s.reshape(nb, nt, N_HEADS_SB, 1, HEAD_DIM)
    eye = jnp.eye(N_HEADS_SB, dtype=BF16)[None, None, :, :, None]
    qbd = (qb4 * eye).reshape(nb, nt * N_HEADS_SB, Q_SB)
    new_sb = sb_new.reshape(nb, nt, 2, N_HEADS_SB, HEAD_DIM).transpose(2, 0, 3, 4, 1)
    new_sb = _pad_axis(new_sb, 4, PAGE).reshape(2, nb, Q_SB, PAGE).astype(BF16)
    ob_s = _sb_sample_call(page_table, qbd, new_sb[0], new_sb[1], sb_cache, nt)
    ob_s = ob_s.reshape(rs, Q_SB).astype(BF16)

    x1s, h2s = _merge_call(oa_s, ob_s, mg_s, wa, wb, wo, xs, mods[2], ng[1:2], ng[2:3], mods[3], mods[4], tm=rs)
    y_sample = _ffn_call(h2s, wup, wdn, x1s, mods[5], ng[3:4], tm=rs)

    win_new5 = win_new.reshape(nb, nt, 2, N_KV_NSA, HEAD_DIM)
    win_kv_sample = jnp.concatenate([state_win_kv[0], win_new5], axis=1)[:, nt:][:, -WINDOW:]
    return (y_prompt[None],
            y_sample.reshape(nb, nt, D_MODEL),
            nsa_rows.reshape(1, 1, t, 4, N_KV_NSA, HEAD_DIM),
            sb_rows.reshape(1, 1, t, 2, N_HEADS_SB, HEAD_DIM),
            win_rows[t - WINDOW:].reshape(1, 1, WINDOW, 2, N_KV_NSA, HEAD_DIM),
            nsa_new.reshape(1, nb, nt, 4, N_KV_NSA, HEAD_DIM),
            sb_new.reshape(1, nb, nt, 2, N_HEADS_SB, HEAD_DIM),
            win_kv_sample[None])
```
